```python
import jax, jax.numpy as jnp
from jax import lax
import numpy as np

D_MODEL = 1024
BATCH = 2
SEQ = 8192
DEPTH = 2
DEC_BATCH = 32
DEC_SEQ = 2048
PAST_LEN = 128

GRID_W = 64
NA_HEADS = 8
NA_DH = 64
NA_W = NA_HEADS * NA_DH
NA_KH = 8
NA_KW = 16
NA_SPAN = 2 * NA_KW
DN_HEADS = 4
DN_DK = 128
DN_DV = 128
DN_QK_W = DN_HEADS * DN_DK
DN_W = DN_HEADS * DN_DV
CONV_W = 5
CHUNK = 64
D_FF = 4 * D_MODEL
N_MOD = 6
EPS = 1e-6
IN_SIZES = (3 * NA_W, 2 * DN_QK_W + DN_W, DN_W, 4 * DN_HEADS, 2 * D_MODEL)
IN_COLS = 3 * NA_W + 2 * DN_QK_W + DN_W + DN_W + 4 * DN_HEADS + 2 * D_MODEL

kernel_name = "hybrid_natten_gdn_encoder"


def _rmsnorm(x, g):
    xf = x.astype(jnp.float32)
    y = xf * lax.rsqrt(jnp.mean(xf * xf, axis=-1, keepdims=True) + EPS)
    return (y * g.astype(jnp.float32)).astype(x.dtype)


def _l2norm(x):
    return x * lax.rsqrt(jnp.sum(x * x, axis=-1, keepdims=True) + EPS)


def _neighborhood_attention(q, k, v, rpb):
    B, L, H, Dh = q.shape
    rows = L // GRID_W
    kh = min(NA_KH, rows)
    n_cb = GRID_W // NA_KW
    scale = Dh ** -0.5
    qg = q.reshape(B, rows, GRID_W, H, Dh)
    kg = k.reshape(B, rows, GRID_W, H, Dh)
    vg = v.reshape(B, rows, GRID_W, H, Dh)
    cols = np.arange(GRID_W)
    col_start = np.clip(cols - NA_KW // 2, 0, GRID_W - NA_KW)
    cb0 = np.arange(n_cb) * NA_KW
    kc0 = np.clip(cb0 - NA_KW // 2, 0, GRID_W - NA_SPAN)
    kcols = kc0[:, None] + np.arange(NA_SPAN)[None, :]
    qcols = cb0[:, None] + np.arange(NA_KW)[None, :]
    qs = col_start[qcols]
    valid = (kcols[:, None, :] >= qs[..., None]) & (kcols[:, None, :] < qs[..., None] + NA_KW)
    neg = jnp.asarray(np.where(valid, 0.0, -1e30).astype(np.float32))
    dc_idx = np.clip(kcols[:, None, :] - qcols[:, :, None] + NA_KW - 1, 0, 2 * NA_KW - 2)
    rpb_c = rpb.astype(jnp.float32)[:, :, dc_idx]

    def one_row(r):
        sr = jnp.clip(r - kh // 2, 0, rows - kh)
        qr = lax.dynamic_index_in_dim(qg, r, axis=1, keepdims=False)
        kb = lax.dynamic_slice_in_dim(kg, sr, kh, axis=1)[:, :, kcols]
        vb = lax.dynamic_slice_in_dim(vg, sr, kh, axis=1)[:, :, kcols]
        qr = qr.reshape(B, n_cb, NA_KW, H, Dh)
        s = jnp.einsum('bcqhd,bicjhd->bhcqij', qr, kb).astype(jnp.float32) * scale
        dr_idx = sr + jnp.arange(kh) - r + (NA_KH - 1)
        bias = jnp.take(rpb_c, dr_idx, axis=1).transpose(0, 2, 3, 1, 4)
        s = s + bias[None] + neg[:, :, None, :]
        p = jax.nn.softmax(s.reshape(s.shape[:4] + (kh * NA_SPAN,)), axis=-1).reshape(s.shape)
        o = jnp.einsum('bhcqij,bicjhd->bcqhd', p.astype(vb.dtype), vb)
        return o.reshape(B, GRID_W, H, Dh)

    out = lax.map(one_row, jnp.arange(rows))
    return out.transpose(1, 0, 2, 3, 4).reshape(B, L, H * Dh)


def _gated_delta_chunked(q, k, v, g, beta):
    B, L, H, Dk = q.shape
    Dv = v.shape[-1]
    n = L // CHUNK
    f32 = jnp.float32
    to_c = lambda t: t.astype(f32).reshape((B, n, CHUNK, H) + t.shape[3:]).swapaxes(2, 3).swapaxes(1, 2)
    qc, kc, vc = to_c(q), to_c(k), to_c(v)
    gc, bc = to_c(g), to_c(beta)
    G = jnp.cumsum(gc, axis=-1)
    diff = G[..., :, None] - G[..., None, :]
    incl = np.tril(np.ones((CHUNK, CHUNK), dtype=bool))
    strict = np.tril(np.ones((CHUNK, CHUNK), dtype=bool), -1)
    decay_in = jnp.exp(jnp.where(incl, diff, -jnp.inf))
    kk = jnp.einsum('bhnik,bhnjk->bhnij', kc, kc)
    A = jnp.where(strict, bc[..., :, None] * kk * decay_in, 0.0)
    T = A + jnp.eye(CHUNK, dtype=f32)
    W = lax.linalg.triangular_solve(T, bc[..., None] * jnp.exp(G)[..., None] * kc,
                                    left_side=True, lower=True, unit_diagonal=True)
    U = lax.linalg.triangular_solve(T, bc[..., None] * vc,
                                    left_side=True, lower=True, unit_diagonal=True)
    qk = jnp.einsum('bhnik,bhnjk->bhnij', qc, kc) * decay_in
    q_dec = qc * jnp.exp(G)[..., None]
    k_dec = kc * jnp.exp(G[..., -1:] - G)[..., None]
    g_last = jnp.exp(G[..., -1])
    xs = (jnp.moveaxis(W, 2, 0), jnp.moveaxis(U, 2, 0), jnp.moveaxis(qk, 2, 0),
          jnp.moveaxis(q_dec, 2, 0), jnp.moveaxis(k_dec, 2, 0), jnp.moveaxis(g_last, 2, 0))

    def step(S, inp):
        w_c, u_c, qk_c, qd_c, kd_c, gl_c = inp
        v_new = u_c - jnp.einsum('bhck,bhkv->bhcv', w_c, S)
        o = jnp.einsum('bhck,bhkv->bhcv', qd_c, S) + jnp.einsum('bhij,bhjv->bhiv', qk_c, v_new)
        S = S * gl_c[..., None, None] + jnp.einsum('bhck,bhcv->bhkv', kd_c, v_new)
        return S, o

    S0 = jnp.zeros((B, H, Dk, Dv), f32)
    _, o = lax.scan(step, S0, xs)
    return o.transpose(1, 0, 3, 2, 4).reshape(B, L, H, Dv)


def _bidir_delta(q, k, v, g, beta):
    flip = lambda t: jnp.flip(t, axis=1)
    o_f = _gated_delta_chunked(q, k, v, g[:, :, 0], beta[:, :, 0])
    o_b = flip(_gated_delta_chunked(flip(q), flip(k), flip(v), flip(g[:, :, 1]), flip(beta[:, :, 1])))
    return o_f + o_b


def _centred_conv(x, w):
    C = x.shape[-1]
    return lax.conv_general_dilated(x, w[:, None, :].astype(x.dtype), window_strides=(1,),
                                    padding=[(CONV_W // 2, CONV_W // 2)],
                                    dimension_numbers=('NWC', 'WIO', 'NWC'),
                                    feature_group_count=C)


def _mixer(h, w_in, rpb, conv_w, a_log, dt_bias, dn_norm_g, w_br_attn, w_br_dn, w_out):
    B, L, _ = h.shape
    proj = h @ w_in
    na_qkv, dn_qkv, z, ab, gates = jnp.split(proj, np.cumsum(IN_SIZES)[:-1].tolist(), axis=-1)
    qa, ka, va = jnp.split(na_qkv.reshape(B, L, 3, NA_HEADS, NA_DH), 3, axis=2)
    o_a = _neighborhood_attention(qa[:, :, 0], ka[:, :, 0], va[:, :, 0], rpb)
    dn_qkv = jax.nn.silu(_centred_conv(dn_qkv, conv_w))
    qd, kd, vd = jnp.split(dn_qkv, [DN_QK_W, 2 * DN_QK_W], axis=-1)
    qd = _l2norm(qd.reshape(B, L, DN_HEADS, DN_DK).astype(jnp.float32)) * (DN_DK ** -0.5)
    kd = _l2norm(kd.reshape(B, L, DN_HEADS, DN_DK).astype(jnp.float32))
    vd = vd.reshape(B, L, DN_HEADS, DN_DV)
    ab = ab.reshape(B, L, 4, DN_HEADS).astype(jnp.float32)
    beta = jax.nn.sigmoid(ab[:, :, 0:2])
    g = -jnp.exp(a_log.astype(jnp.float32)) * jax.nn.softplus(ab[:, :, 2:4] + dt_bias.astype(jnp.float32))
    o_d = _bidir_delta(qd, kd, vd, g, beta)
    o_d = _rmsnorm(o_d, dn_norm_g) * jax.nn.silu(z.reshape(B, L, DN_HEADS, DN_DV).astype(jnp.float32))
    o_d = o_d.reshape(B, L, DN_W).astype(h.dtype)
    g_a, g_d = jnp.split(jax.nn.sigmoid(gates), 2, axis=-1)
    merged = g_a * (o_a @ w_br_attn) + g_d * (o_d @ w_br_dn)
    return merged @ w_out


def _trunk(x, c, norm_mix_g, norm_mlp_g, w_ada, b_ada, w_in, na_rpb, dn_conv, dn_a_log,
           dn_dt_bias, dn_norm_g, w_br_attn, w_br_dn, w_out, w_mlp1, w_mlp2, final_norm_g):
    c_act = jax.nn.silu(c)
    for l in range(DEPTH):
        mod = (c_act @ w_ada[l] + b_ada[l])[:, None, :]
        sh1, sc1, gt1, sh2, sc2, gt2 = jnp.split(mod, N_MOD, axis=-1)
        h = _rmsnorm(x, norm_mix_g[l]) * (1 + sc1) + sh1
        x = x + gt1 * _mixer(h, w_in[l], na_rpb[l], dn_conv[l], dn_a_log[l], dn_dt_bias[l],
                             dn_norm_g[l], w_br_attn[l], w_br_dn[l], w_out[l])
        h = _rmsnorm(x, norm_mlp_g[l]) * (1 + sc2) + sh2
        x = x + gt2 * (jnp.square(jax.nn.relu(h @ w_mlp1[l])) @ w_mlp2[l])
    return _rmsnorm(x, final_norm_g)


def setup_inputs(seed: int = 0) -> dict:
    key = jax.random.key(seed)
    ks = jax.random.split(key, 20)
    nrm = lambda k, shape, s: jax.random.normal(k, shape, jnp.float32) * s
    return {
        "x_prompt": nrm(ks[0], (BATCH, SEQ, D_MODEL), 1.0),
        "x_sample": nrm(ks[1], (DEC_BATCH, DEC_SEQ, D_MODEL), 1.0),
        "c_prompt": nrm(ks[2], (BATCH, D_MODEL), 1.0),
        "c_sample": nrm(ks[3], (DEC_BATCH, D_MODEL), 1.0),
        "norm_mix_g": 1.0 + nrm(ks[4], (DEPTH, D_MODEL), 0.05),
        "norm_mlp_g": 1.0 + nrm(ks[5], (DEPTH, D_MODEL), 0.05),
        "w_ada": nrm(ks[6], (DEPTH, D_MODEL, N_MOD * D_MODEL), 0.5 * D_MODEL ** -0.5),
        "b_ada": nrm(ks[7], (DEPTH, N_MOD * D_MODEL), 0.02),
        "w_in": nrm(ks[8], (DEPTH, D_MODEL, IN_COLS), D_MODEL ** -0.5),
        "na_rpb": nrm(ks[9], (DEPTH, NA_HEADS, 2 * NA_KH - 1, 2 * NA_KW - 1), 0.1),
        "dn_conv": nrm(ks[10], (DEPTH, CONV_W, 2 * DN_QK_W + DN_W), CONV_W ** -0.5),
        "dn_a_log": jnp.log(jax.random.uniform(ks[11], (DEPTH, 2, DN_HEADS), jnp.float32, 1.0, 16.0)),
        "dn_dt_bias": jnp.log(jnp.expm1(jax.random.uniform(ks[12], (DEPTH, 2, DN_HEADS), jnp.float32, 0.001, 0.1))),
        "dn_norm_g": 1.0 + nrm(ks[13], (DEPTH, DN_DV), 0.05),
        "w_br_attn": nrm(ks[14], (DEPTH, NA_W, D_MODEL), NA_W ** -0.5),
        "w_br_dn": nrm(ks[15], (DEPTH, DN_W, D_MODEL), DN_W ** -0.5),
        "w_out": nrm(ks[16], (DEPTH, D_MODEL, D_MODEL), D_MODEL ** -0.5),
        "w_mlp1": nrm(ks[17], (DEPTH, D_MODEL, D_FF), D_MODEL ** -0.5),
        "w_mlp2": nrm(ks[18], (DEPTH, D_FF, D_MODEL), D_FF ** -0.5),
        "final_norm_g": 1.0 + nrm(ks[19], (D_MODEL,), 0.05),
    }


def reference(x_prompt, x_sample, c_prompt, c_sample, norm_mix_g, norm_mlp_g, w_ada, b_ada,
              w_in, na_rpb, dn_conv, dn_a_log, dn_dt_bias, dn_norm_g, w_br_attn, w_br_dn,
              w_out, w_mlp1, w_mlp2, final_norm_g):
    y_prompt = _trunk(x_prompt, c_prompt, norm_mix_g, norm_mlp_g, w_ada, b_ada, w_in, na_rpb,
                      dn_conv, dn_a_log, dn_dt_bias, dn_norm_g, w_br_attn, w_br_dn, w_out,
                      w_mlp1, w_mlp2, final_norm_g)
    y_sample = _trunk(x_sample, c_sample, norm_mix_g, norm_mlp_g, w_ada, b_ada, w_in, na_rpb,
                      dn_conv, dn_a_log, dn_dt_bias, dn_norm_g, w_br_attn, w_br_dn, w_out,
                      w_mlp1, w_mlp2, final_norm_g)
    return (y_prompt, y_sample)
```

```python
import functools

import jax
import jax.numpy as jnp
import numpy as np
from jax import lax
from jax.experimental import pallas as pl
from jax.experimental.pallas import tpu as pltpu

D_MODEL = 1024
DEPTH = 2
GRID_W = 64
NA_HEADS = 8
NA_DH = 64
NA_W = NA_HEADS * NA_DH
NA_KH = 8
NA_KW = 16
DN_HEADS = 4
DN_DK = 128
DN_DV = 128
DN_QK_W = DN_HEADS * DN_DK
DN_W = DN_HEADS * DN_DV
DN_QKV_W = 2 * DN_QK_W + DN_W
CONV_W = 5
CHUNK = 64
D_FF = 4 * D_MODEL
N_MOD = 6
EPS = 1e-6
N_GATE = 2 * D_MODEL
AB_W = 4 * DN_HEADS

LANES = 128
SUBLANES = 8
VMEM_LIMIT_BYTES = 56 * 1024 * 1024

TM_IN = 256
TM_POST = 256
TM_PREP = 512
ATT_ROWS = 8
ATT_WIN_ROWS = 3 * ATT_ROWS
DN_GROUP = 2
HALO = SUBLANES

BF16 = jnp.bfloat16
F32 = jnp.float32
HIGHEST = lax.Precision.HIGHEST


def _compiler_params(semantics):
    return pltpu.CompilerParams(dimension_semantics=semantics,
                                vmem_limit_bytes=VMEM_LIMIT_BYTES)


def _resident(block_shape, index_map):
    return pl.BlockSpec(block_shape, index_map, pipeline_mode=pl.Buffered(1))


def _silu(x):
    return x * (1.0 / (1.0 + jnp.exp(-x)))


def _sigmoid(x):
    return 1.0 / (1.0 + jnp.exp(-x))


def _rms(x, g):
    return (x * lax.rsqrt(jnp.mean(x * x, axis=-1, keepdims=True) + EPS)) * g


def _mod_kernel(c_ref, w_ref, b_ref, o_ref):
    c = _silu(c_ref[...])
    o_ref[...] = jnp.dot(c, w_ref[...], precision=HIGHEST,
                         preferred_element_type=F32) + b_ref[...]


def _modulation(c, w_ada, b_ada):
    nb = c.shape[0]
    tn = 1024
    n_out = N_MOD * D_MODEL
    return pl.pallas_call(
        _mod_kernel,
        grid=(DEPTH, n_out // tn),
        in_specs=[
            pl.BlockSpec((nb, D_MODEL), lambda l, j: (0, 0)),
            pl.BlockSpec((None, D_MODEL, tn), lambda l, j: (l, 0, j)),
            pl.BlockSpec((None, 1, tn), lambda l, j: (l, 0, j)),
        ],
        out_specs=pl.BlockSpec((None, nb, tn), lambda l, j: (l, 0, j)),
        out_shape=jax.ShapeDtypeStruct((DEPTH, nb, n_out), F32),
        compiler_params=_compiler_params(("arbitrary", "arbitrary")),
        name="adaln_mod",
    )(c, w_ada, b_ada.reshape(DEPTH, 1, n_out))


def _in_kernel(x_ref, mod_ref, g_ref, wa_ref, wd_ref, wz_ref, wab_ref, wg_ref,
               qa_ref, ka_ref, va_ref, dqkv_ref, z_ref, ab_ref, gate_ref):
    x = x_ref[...]
    sh1 = mod_ref[:, 0:D_MODEL]
    sc1 = mod_ref[:, D_MODEL:2 * D_MODEL]
    h = (_rms(x, g_ref[...]) * (1.0 + sc1) + sh1).astype(BF16)
    na = jnp.dot(h, wa_ref[...], preferred_element_type=F32)
    qa_ref[...] = (na[:, 0:NA_W] * (NA_DH ** -0.5)).astype(BF16)
    ka_ref[...] = na[:, NA_W:2 * NA_W].astype(BF16)
    va_ref[...] = na[:, 2 * NA_W:3 * NA_W].astype(BF16)
    dqkv_ref[...] = jnp.dot(h, wd_ref[...], preferred_element_type=F32)
    z_ref[...] = jnp.dot(h, wz_ref[...], preferred_element_type=F32)
    ab_ref[...] = jnp.dot(h, wab_ref[...], preferred_element_type=F32)
    gate_ref[...] = jnp.dot(h, wg_ref[...], preferred_element_type=F32)


def _in_projection(x, mod_l, norm_g, w_parts):
    nb, seq, _ = x.shape
    tm = TM_IN
    wa, wd, wz, wab, wg = w_parts
    tok = lambda width: pl.BlockSpec((None, tm, width), lambda b, i: (b, i, 0))
    full = lambda w: _resident(w.shape, lambda b, i: (0, 0))
    out_shape = (
        jax.ShapeDtypeStruct((nb, seq, NA_W), BF16),
        jax.ShapeDtypeStruct((nb, seq, NA_W), BF16),
        jax.ShapeDtypeStruct((nb, seq, NA_W), BF16),
        jax.ShapeDtypeStruct((nb, seq, DN_QKV_W), F32),
        jax.ShapeDtypeStruct((nb, seq, DN_W), F32),
        jax.ShapeDtypeStruct((nb, seq, LANES), F32),
        jax.ShapeDtypeStruct((nb, seq, N_GATE), F32),
    )
    return pl.pallas_call(
        _in_kernel,
        grid=(nb, seq // tm),
        in_specs=[
            tok(D_MODEL),
            pl.BlockSpec((None, 1, N_MOD * D_MODEL), lambda b, i: (b, 0, 0)),
            _resident((1, D_MODEL), lambda b, i: (0, 0)),
            full(wa), full(wd), full(wz), full(wab), full(wg),
        ],
        out_specs=(tok(NA_W), tok(NA_W), tok(NA_W), tok(DN_QKV_W), tok(DN_W),
                   tok(LANES), tok(N_GATE)),
        out_shape=out_shape,
        compiler_params=_compiler_params(("parallel", "parallel")),
        name="in_projection",
    )(x, mod_l, norm_g, wa, wd, wz, wab, wg)


def _att_kernel(q_ref, k_ref, v_ref, bias_ref, o_ref, *, n_blocks):
    j = pl.program_id(1)
    rows = n_blocks * ATT_ROWS
    base = jnp.clip(j - 1, 0, n_blocks - 3)
    lane = lax.broadcasted_iota(jnp.int32, (GRID_W, LANES), 1)
    low = lane < NA_DH

    def row_body(ri, carry):
        r = j * ATT_ROWS + ri
        sr = jnp.clip(r - NA_KH // 2, 0, rows - NA_KH)
        delta = r - sr
        off = pl.multiple_of((sr - base * ATT_ROWS) * GRID_W, GRID_W)
        qrow = pl.multiple_of(ri * GRID_W, GRID_W)
        for p in range(NA_HEADS // 2):
            lanes = slice(p * LANES, (p + 1) * LANES)
            qp = q_ref[pl.ds(qrow, GRID_W), lanes]
            zero = jnp.zeros_like(qp)
            lhs = jnp.concatenate([jnp.where(low, qp, zero), jnp.where(low, zero, qp)], axis=0)
            kp = k_ref[0, pl.ds(off, NA_KH * GRID_W), lanes]
            vp = v_ref[0, pl.ds(off, NA_KH * GRID_W), lanes]
            s = lax.dot_general(lhs, kp, (((1,), (1,)), ((), ())), preferred_element_type=F32)
            s = s + bias_ref[delta, p]
            m = jnp.max(s, axis=-1, keepdims=True)
            e = jnp.exp(s - m)
            den = jnp.sum(e, axis=-1, keepdims=True)
            pv = jnp.dot(e.astype(BF16), vp, preferred_element_type=F32) / den
            o_ref[pl.ds(qrow, GRID_W), lanes] = jnp.where(low, pv[0:GRID_W], pv[GRID_W:]).astype(BF16)
        return carry

    lax.fori_loop(0, ATT_ROWS, row_body, 0)


def _attention(qa, ka, va, bias):
    nb, seq, _ = qa.shape
    blk = ATT_ROWS * GRID_W
    n_blocks = seq // blk
    assert n_blocks >= 3
    win = ATT_WIN_ROWS * GRID_W

    def win_map(b, j):
        return (b, jnp.clip(j - 1, 0, n_blocks - 3) * blk, 0)

    win_spec = pl.BlockSpec((pl.Element(1), pl.Element(win), pl.Element(NA_W)), win_map)
    return pl.pallas_call(
        functools.partial(_att_kernel, n_blocks=n_blocks),
        grid=(nb, n_blocks),
        in_specs=[
            pl.BlockSpec((None, blk, NA_W), lambda b, j: (b, j, 0)),
            win_spec, win_spec,
            _resident(bias.shape, lambda b, j: (0, 0, 0, 0)),
        ],
        out_specs=pl.BlockSpec((None, blk, NA_W), lambda b, j: (b, j, 0)),
        out_shape=jax.ShapeDtypeStruct((nb, seq, NA_W), BF16),
        compiler_params=_compiler_params(("parallel", "parallel")),
        name="neighborhood_attention",
    )(qa, ka, va, bias)


def _attention_bias(rpb):
    qc = np.arange(GRID_W)
    kc = np.arange(GRID_W)
    start = np.clip(qc - NA_KW // 2, 0, GRID_W - NA_KW)
    valid = (kc[None, :] >= start[:, None]) & (kc[None, :] < start[:, None] + NA_KW)
    neg = jnp.asarray(np.where(valid, 0.0, -1e30).astype(np.float32))
    dc = np.clip(kc[None, :] - qc[:, None] + NA_KW - 1, 0, 2 * NA_KW - 2)
    delta = np.arange(NA_KH)
    dr = np.arange(NA_KH)[None, :] - delta[:, None] + (NA_KH - 1)
    t = rpb.astype(F32)[:, dr][:, :, :, dc]
    t = t + neg[None, None, None]
    t = t.transpose(1, 0, 3, 2, 4).reshape(NA_KH, NA_HEADS // 2, 2 * GRID_W, NA_KH * GRID_W)
    return t


def _prep_kernel(x_ref, prev_ref, next_ref, ab_ref, cw_ref, alog_ref, dtb_ref,
                 q_ref, k_ref, v_ref, gb_ref, pad_ref, *, tm):
    i = pl.program_id(1)
    n = pl.num_programs(1)
    pad_ref[0:HALO, :] = jnp.where(i > 0, prev_ref[...], 0.0)
    pad_ref[HALO:HALO + tm, :] = x_ref[...]
    pad_ref[HALO + tm:, :] = jnp.where(i < n - 1, next_ref[...], 0.0)
    first = HALO - CONV_W // 2
    acc = pad_ref[first:first + tm, :] * cw_ref[0:1, :]
    for t in range(1, CONV_W):
        acc = acc + pad_ref[first + t:first + t + tm, :] * cw_ref[t:t + 1, :]
    y = _silu(acc)
    for h in range(DN_HEADS):
        ql = slice(h * DN_DK, (h + 1) * DN_DK)
        qh = y[:, ql]
        qn = qh * lax.rsqrt(jnp.sum(qh * qh, axis=-1, keepdims=True) + EPS)
        q_ref[:, ql] = qn * (DN_DK ** -0.5)
        kh = y[:, DN_QK_W + h * DN_DK:DN_QK_W + (h + 1) * DN_DK]
        k_ref[:, ql] = kh * lax.rsqrt(jnp.sum(kh * kh, axis=-1, keepdims=True) + EPS)
    v_ref[...] = y[:, 2 * DN_QK_W:]
    ab = ab_ref[...]
    beta = _sigmoid(ab)
    a = ab + dtb_ref[...]
    softplus = jnp.maximum(a, 0.0) + jnp.log1p(jnp.exp(-jnp.abs(a)))
    g = -jnp.exp(alog_ref[...]) * softplus
    col = lax.broadcasted_iota(jnp.int32, ab.shape, 1)
    gb_ref[...] = jnp.where(col < 2 * DN_HEADS, beta, jnp.where(col < AB_W, g, 0.0))


def _dn_prep(dqkv, ab, conv_w, a_log, dt_bias):
    nb, seq, _ = dqkv.shape
    tm = min(TM_PREP, seq)
    per = tm // HALO
    n_halo = seq // HALO
    pad_cols = lambda t: jnp.pad(t.reshape(1, 2 * DN_HEADS).astype(F32),
                                 ((0, 0), (2 * DN_HEADS, LANES - AB_W)))
    tok = lambda width: pl.BlockSpec((None, tm, width), lambda b, i: (b, i, 0))
    return pl.pallas_call(
        functools.partial(_prep_kernel, tm=tm),
        grid=(nb, seq // tm),
        in_specs=[
            tok(DN_QKV_W),
            pl.BlockSpec((None, HALO, DN_QKV_W), lambda b, i: (b, jnp.maximum(i * per - 1, 0), 0)),
            pl.BlockSpec((None, HALO, DN_QKV_W),
                         lambda b, i: (b, jnp.minimum((i + 1) * per, n_halo - 1), 0)),
            tok(LANES),
            _resident((CONV_W, DN_QKV_W), lambda b, i: (0, 0)),
            _resident((1, LANES), lambda b, i: (0, 0)),
            _resident((1, LANES), lambda b, i: (0, 0)),
        ],
        out_specs=(tok(DN_QK_W), tok(DN_QK_W), tok(DN_W), tok(LANES)),
        out_shape=(
            jax.ShapeDtypeStruct((nb, seq, DN_QK_W), F32),
            jax.ShapeDtypeStruct((nb, seq, DN_QK_W), F32),
            jax.ShapeDtypeStruct((nb, seq, DN_W), F32),
            jax.ShapeDtypeStruct((nb, seq, LANES), F32),
        ),
        scratch_shapes=[pltpu.VMEM((tm + 2 * HALO, DN_QKV_W), F32)],
        compiler_params=_compiler_params(("parallel", "parallel")),
        name="deltanet_prep",
    )(dqkv, dqkv, dqkv, ab, conv_w, pad_cols(a_log), pad_cols(dt_bias))


def _solve_unit_triangular(a, rhs):
    x = rhs
    sign = -1.0
    n_levels = 6
    for level in range(n_levels):
        a_b = a.astype(BF16)
        if level + 1 < n_levels:
            prod = jnp.dot(a_b, jnp.concatenate([x, a], axis=1).astype(BF16),
                           preferred_element_type=F32)
            ax, a = prod[:, :x.shape[1]], prod[:, x.shape[1]:]
        else:
            ax = jnp.dot(a_b, x.astype(BF16), preferred_element_type=F32)
        x = x + sign * ax
        sign = 1.0
    return x


def _delta_chunk(q, k, v, gb, s_ref, h, reverse):
    row = lax.broadcasted_iota(jnp.int32, (CHUNK, CHUNK), 0)
    colm = lax.broadcasted_iota(jnp.int32, (CHUNK, CHUNK), 1)
    if reverse:
        incl, strict = colm >= row, colm > row
    else:
        incl, strict = colm <= row, colm < row
    c_beta = h + (DN_HEADS if reverse else 0)
    c_g = 2 * DN_HEADS + c_beta
    beta = gb["raw"][:, c_beta:c_beta + 1]
    g_col = gb["cum_col"][:, c_g:c_g + 1]
    g_row = gb["cum_row"][c_g:c_g + 1, :]
    g_tot = gb["total"][:, c_g:c_g + 1]
    decay = jnp.exp(jnp.where(incl, g_col - g_row, -jnp.inf))
    q_b, k_b = q.astype(BF16), k.astype(BF16)
    qk_kk = lax.dot_general(jnp.concatenate([q_b, k_b], axis=0), k_b,
                            (((1,), (1,)), ((), ())), preferred_element_type=F32)
    qk = qk_kk[:CHUNK] * decay
    a = jnp.where(strict, beta * qk_kk[CHUNK:] * decay, 0.0)
    exp_g = jnp.exp(g_col)
    rhs = jnp.concatenate([(beta * exp_g) * k, beta * v], axis=1)
    wu = _solve_unit_triangular(a, rhs)
    w, u = wu[:, :DN_DK], wu[:, DN_DK:]
    q_dec = q * exp_g
    k_dec = k * jnp.exp(g_tot - g_col)
    s = s_ref[h]
    ws_qs = jnp.dot(jnp.concatenate([w, q_dec], axis=0).astype(BF16), s.astype(BF16),
                    preferred_element_type=F32)
    v_new = u - ws_qs[:CHUNK]
    v_new_b = v_new.astype(BF16)
    o = ws_qs[CHUNK:] + jnp.dot(qk.astype(BF16), v_new_b, preferred_element_type=F32)
    ds = lax.dot_general(k_dec.astype(BF16), v_new_b, (((0,), (0,)), ((), ())),
                         preferred_element_type=F32)
    g_last = jnp.exp(jnp.concatenate([g_tot, g_tot], axis=0))
    s_ref[h] = s * g_last + ds
    return o


def _dn_kernel(qf_ref, kf_ref, vf_ref, gf_ref, qb_ref, kb_ref, vb_ref, gbb_ref,
               of_ref, ob_ref, sf_ref, sb_ref):
    @pl.when(pl.program_id(1) == 0)
    def _():
        sf_ref[...] = jnp.zeros_like(sf_ref)
        sb_ref[...] = jnp.zeros_like(sb_ref)

    row = lax.broadcasted_iota(jnp.int32, (CHUNK, CHUNK), 0)
    colm = lax.broadcasted_iota(jnp.int32, (CHUNK, CHUNK), 1)
    lower = (colm <= row).astype(F32)
    upper = (colm >= row).astype(F32)
    ones = jnp.ones((CHUNK, CHUNK), F32)

    def tables(graw, reverse):
        tri, tri_t = (upper, lower) if reverse else (lower, upper)
        return {
            "raw": graw,
            "cum_col": jnp.dot(tri, graw, precision=HIGHEST, preferred_element_type=F32),
            "cum_row": lax.dot_general(graw, tri_t, (((0,), (0,)), ((), ())),
                                       precision=HIGHEST, preferred_element_type=F32),
            "total": jnp.dot(ones, graw, precision=HIGHEST, preferred_element_type=F32),
        }

    for step in range(DN_GROUP):
        for reverse in (False, True):
            c = DN_GROUP - 1 - step if reverse else step
            q_ref, k_ref, v_ref, g_ref, o_ref, s_ref = (
                (qb_ref, kb_ref, vb_ref, gbb_ref, ob_ref, sb_ref) if reverse
                else (qf_ref, kf_ref, vf_ref, gf_ref, of_ref, sf_ref))
            rows = slice(c * CHUNK, (c + 1) * CHUNK)
            gb = tables(g_ref[rows, :], reverse)
            for h in range(DN_HEADS):
                lanes = slice(h * DN_DK, (h + 1) * DN_DK)
                o_ref[rows, lanes] = _delta_chunk(q_ref[rows, lanes], k_ref[rows, lanes],
                                                  v_ref[rows, lanes], gb, s_ref, h, reverse)


def _delta_rule(qn, kn, v, gb):
    nb, seq, _ = qn.shape
    tg = DN_GROUP * CHUNK
    n = seq // tg
    fwd = lambda width: pl.BlockSpec((None, tg, width), lambda b, i: (b, i, 0))
    bwd = lambda width: pl.BlockSpec((None, tg, width), lambda b, i: (b, n - 1 - i, 0))
    state = pltpu.VMEM((DN_HEADS, DN_DK, DN_DV), F32)
    return pl.pallas_call(
        _dn_kernel,
        grid=(nb, n),
        in_specs=[fwd(DN_QK_W), fwd(DN_QK_W), fwd(DN_W), fwd(LANES),
                  bwd(DN_QK_W), bwd(DN_QK_W), bwd(DN_W), bwd(LANES)],
        out_specs=(fwd(DN_W), bwd(DN_W)),
        out_shape=(jax.ShapeDtypeStruct((nb, seq, DN_W), F32),
                   jax.ShapeDtypeStruct((nb, seq, DN_W), F32)),
        scratch_shapes=[state, state],
        compiler_params=_compiler_params(("parallel", "arbitrary")),
        name="delta_rule",
    )(qn, kn, v, gb, qn, kn, v, gb)


def _post_kernel(x_ref, oa_ref, of_ref, ob_ref, z_ref, gate_ref, mod_ref, dng_ref, g2_ref,
                 fg_ref, wba_ref, wbd_ref, wo_ref, w1_ref, w2_ref, y_ref, *, final):
    od = of_ref[...] + ob_ref[...]
    z = z_ref[...]
    dng = dng_ref[...]
    parts = []
    for h in range(DN_HEADS):
        lanes = slice(h * DN_DV, (h + 1) * DN_DV)
        parts.append(_rms(od[:, lanes], dng) * _silu(z[:, lanes]))
    od_n = jnp.concatenate(parts, axis=1).astype(BF16)
    br_a = jnp.dot(oa_ref[...], wba_ref[...], preferred_element_type=F32)
    br_d = jnp.dot(od_n, wbd_ref[...], preferred_element_type=F32)
    merged = _sigmoid(gate_ref[:, 0:D_MODEL]) * br_a + _sigmoid(gate_ref[:, D_MODEL:]) * br_d
    mix = jnp.dot(merged.astype(BF16), wo_ref[...], preferred_element_type=F32)
    gt1 = mod_ref[:, 2 * D_MODEL:3 * D_MODEL]
    sh2 = mod_ref[:, 3 * D_MODEL:4 * D_MODEL]
    sc2 = mod_ref[:, 4 * D_MODEL:5 * D_MODEL]
    gt2 = mod_ref[:, 5 * D_MODEL:6 * D_MODEL]
    x = x_ref[...] + gt1 * mix
    h2 = (_rms(x, g2_ref[...]) * (1.0 + sc2) + sh2).astype(BF16)
    acc = jnp.zeros_like(x)
    for j in range(D_FF // D_MODEL):
        cols = slice(j * D_MODEL, (j + 1) * D_MODEL)
        hid = jnp.maximum(jnp.dot(h2, w1_ref[:, cols], preferred_element_type=F32), 0.0)
        acc = acc + jnp.dot((hid * hid).astype(BF16), w2_ref[cols, :], preferred_element_type=F32)
    x = x + gt2 * acc
    if final:
        x = _rms(x, fg_ref[...])
    y_ref[...] = x


def _post(x, oa, of, ob, z, gates, mod_l, dn_norm_g, norm_mlp_g, final_g, weights, final):
    nb, seq, _ = x.shape
    tm = TM_POST
    tok = lambda width: pl.BlockSpec((None, tm, width), lambda b, i: (b, i, 0))
    vec = lambda width: _resident((1, width), lambda b, i: (0, 0))
    full = lambda w: _resident(w.shape, lambda b, i: (0, 0))
    return pl.pallas_call(
        functools.partial(_post_kernel, final=final),
        grid=(nb, seq // tm),
        in_specs=[
            tok(D_MODEL), tok(NA_W), tok(DN_W), tok(DN_W), tok(DN_W), tok(N_GATE),
            pl.BlockSpec((None, 1, N_MOD * D_MODEL), lambda b, i: (b, 0, 0)),
            vec(DN_DV), vec(D_MODEL), vec(D_MODEL),
        ] + [full(w) for w in weights],
        out_specs=tok(D_MODEL),
        out_shape=jax.ShapeDtypeStruct((nb, seq, D_MODEL), F32),
        compiler_params=_compiler_params(("parallel", "parallel")),
        name="merge_mlp",
    )(x, oa, of, ob, z, gates, mod_l, dn_norm_g, norm_mlp_g, final_g, *weights)


def _split_w_in(w_in_l):
    o1 = 3 * NA_W
    o2 = o1 + DN_QKV_W
    o3 = o2 + DN_W
    o4 = o3 + AB_W
    wab = jnp.pad(w_in_l[:, o3:o4], ((0, 0), (0, LANES - AB_W)))
    parts = (w_in_l[:, :o1], w_in_l[:, o1:o2], w_in_l[:, o2:o3], wab, w_in_l[:, o4:])
    return tuple(p.astype(BF16) for p in parts)


def _trunk(x, mod, p):
    for l in range(DEPTH):
        qa, ka, va, dqkv, z, ab, gates = _in_projection(x, mod[l], p["norm_mix_g"][l], p["w_in"][l])
        oa = _attention(qa, ka, va, p["bias"][l])
        qn, kn, v, gb = _dn_prep(dqkv, ab, p["dn_conv"][l], p["dn_a_log"][l], p["dn_dt_bias"][l])
        of, ob = _delta_rule(qn, kn, v, gb)
        x = _post(x, oa, of, ob, z, gates, mod[l], p["dn_norm_g"][l], p["norm_mlp_g"][l],
                  p["final_norm_g"], p["post_w"][l], final=(l == DEPTH - 1))
    return x


def kernel(x_prompt, x_sample, c_prompt, c_sample, norm_mix_g, norm_mlp_g, w_ada, b_ada, w_in,
           na_rpb, dn_conv, dn_a_log, dn_dt_bias, dn_norm_g, w_br_attn, w_br_dn, w_out, w_mlp1,
           w_mlp2, final_norm_g):
    row = lambda t: t.reshape(1, -1).astype(F32)
    p = {
        "norm_mix_g": [row(norm_mix_g[l]) for l in range(DEPTH)],
        "norm_mlp_g": [row(norm_mlp_g[l]) for l in range(DEPTH)],
        "dn_norm_g": [row(dn_norm_g[l]) for l in range(DEPTH)],
        "final_norm_g": row(final_norm_g),
        "w_in": [_split_w_in(w_in[l]) for l in range(DEPTH)],
        "bias": [_attention_bias(na_rpb[l]) for l in range(DEPTH)],
        "dn_conv": [dn_conv[l].astype(F32) for l in range(DEPTH)],
        "dn_a_log": dn_a_log,
        "dn_dt_bias": dn_dt_bias,
        "post_w": [tuple(w[l].astype(BF16) for w in (w_br_attn, w_br_dn, w_out, w_mlp1, w_mlp2))
                   for l in range(DEPTH)],
    }
    n_prompt = c_prompt.shape[0]
    mod = _modulation(jnp.concatenate([c_prompt, c_sample], axis=0), w_ada, b_ada)
    mod = mod.reshape(DEPTH, -1, 1, N_MOD * D_MODEL)
    y_prompt = _trunk(x_prompt, mod[:, :n_prompt], p)
    y_sample = _trunk(x_sample, mod[:, n_prompt:], p)
    return (y_prompt, y_sample)
```

```python
import functools

import jax
import jax.numpy as jnp
import numpy as np
from jax import lax
from jax.experimental import pallas as pl
from jax.experimental.pallas import tpu as pltpu

D_MODEL = 1024
DEPTH = 2
GRID_W = 64
NA_HEADS = 8
NA_DH = 64
NA_W = NA_HEADS * NA_DH
NA_KH = 8
NA_KW = 16
DN_HEADS = 4
DN_DK = 128
DN_DV = 128
DN_QK_W = DN_HEADS * DN_DK
DN_W = DN_HEADS * DN_DV
DN_QKV_W = 2 * DN_QK_W + DN_W
CONV_W = 5
CHUNK = 64
D_FF = 4 * D_MODEL
N_MOD = 6
EPS = 1e-6
N_GATE = 2 * D_MODEL
AB_W = 4 * DN_HEADS
GB_BETA = 0
GB_G = 2 * DN_HEADS
GB_CUM = 4 * DN_HEADS
GB_TOT = 6 * DN_HEADS
GB_END = 8 * DN_HEADS

LANES = 128
SUBLANES = 8
VMEM_LIMIT_BYTES = 56 * 1024 * 1024

TM_IN = 256
TM_POST = 256
TM_PREP = 512
ATT_ROWS = 8
ATT_WIN_ROWS = 3 * ATT_ROWS
ATT_UNROLL = 2
DN_GROUP = 2
HALO = SUBLANES

BF16 = jnp.bfloat16
F32 = jnp.float32
HIGHEST = lax.Precision.HIGHEST


def _compiler_params(semantics):
    return pltpu.CompilerParams(dimension_semantics=semantics,
                                vmem_limit_bytes=VMEM_LIMIT_BYTES)


def _resident(block_shape, index_map):
    return pl.BlockSpec(block_shape, index_map, pipeline_mode=pl.Buffered(1))


def _silu(x):
    return x * (1.0 / (1.0 + jnp.exp(-x)))


def _sigmoid(x):
    return 1.0 / (1.0 + jnp.exp(-x))


def _rms(x, g):
    return (x * lax.rsqrt(jnp.mean(x * x, axis=-1, keepdims=True) + EPS)) * g


def _mod_kernel(c_ref, w_ref, b_ref, o_ref):
    c = _silu(c_ref[...])
    o_ref[...] = jnp.dot(c, w_ref[...], precision=HIGHEST,
                         preferred_element_type=F32) + b_ref[...]


def _modulation(c, w_ada, b_ada):
    nb = c.shape[0]
    tn = 1024
    n_out = N_MOD * D_MODEL
    return pl.pallas_call(
        _mod_kernel,
        grid=(DEPTH, n_out // tn),
        in_specs=[
            pl.BlockSpec((nb, D_MODEL), lambda l, j: (0, 0)),
            pl.BlockSpec((None, D_MODEL, tn), lambda l, j: (l, 0, j)),
            pl.BlockSpec((None, 1, tn), lambda l, j: (l, 0, j)),
        ],
        out_specs=pl.BlockSpec((None, nb, tn), lambda l, j: (l, 0, j)),
        out_shape=jax.ShapeDtypeStruct((DEPTH, nb, n_out), F32),
        compiler_params=_compiler_params(("arbitrary", "arbitrary")),
        name="adaln_mod",
    )(c, w_ada, b_ada.reshape(DEPTH, 1, n_out))


def _in_kernel(x_ref, mod_ref, g_ref, wa_ref, wd_ref, wz_ref, wab_ref, wg_ref,
               qa_ref, ka_ref, va_ref, dqkv_ref, z_ref, ab_ref, gate_ref):
    x = x_ref[...]
    sh1 = mod_ref[:, 0:D_MODEL]
    sc1 = mod_ref[:, D_MODEL:2 * D_MODEL]
    h = (_rms(x, g_ref[...]) * (1.0 + sc1) + sh1).astype(BF16)
    na = jnp.dot(h, wa_ref[...], preferred_element_type=F32)
    qa_ref[...] = (na[:, 0:NA_W] * (NA_DH ** -0.5)).astype(BF16)
    ka_ref[...] = na[:, NA_W:2 * NA_W].astype(BF16)
    va_ref[...] = na[:, 2 * NA_W:3 * NA_W].astype(BF16)
    dqkv_ref[...] = jnp.dot(h, wd_ref[...], preferred_element_type=F32)
    z_ref[...] = jnp.dot(h, wz_ref[...], preferred_element_type=F32)
    ab_ref[...] = jnp.dot(h, wab_ref[...], preferred_element_type=F32)
    gate_ref[...] = jnp.dot(h, wg_ref[...], preferred_element_type=F32)


def _in_projection(x, mod_l, norm_g, w_parts):
    nb, seq, _ = x.shape
    tm = TM_IN
    wa, wd, wz, wab, wg = w_parts
    tok = lambda width: pl.BlockSpec((None, tm, width), lambda b, i: (b, i, 0))
    full = lambda w: _resident(w.shape, lambda b, i: (0, 0))
    out_shape = (
        jax.ShapeDtypeStruct((nb, seq, NA_W), BF16),
        jax.ShapeDtypeStruct((nb, seq, NA_W), BF16),
        jax.ShapeDtypeStruct((nb, seq, NA_W), BF16),
        jax.ShapeDtypeStruct((nb, seq, DN_QKV_W), F32),
        jax.ShapeDtypeStruct((nb, seq, DN_W), F32),
        jax.ShapeDtypeStruct((nb, seq, LANES), F32),
        jax.ShapeDtypeStruct((nb, seq, N_GATE), F32),
    )
    return pl.pallas_call(
        _in_kernel,
        grid=(nb, seq // tm),
        in_specs=[
            tok(D_MODEL),
            pl.BlockSpec((None, 1, N_MOD * D_MODEL), lambda b, i: (b, 0, 0)),
            _resident((1, D_MODEL), lambda b, i: (0, 0)),
            full(wa), full(wd), full(wz), full(wab), full(wg),
        ],
        out_specs=(tok(NA_W), tok(NA_W), tok(NA_W), tok(DN_QKV_W), tok(DN_W),
                   tok(LANES), tok(N_GATE)),
        out_shape=out_shape,
        compiler_params=_compiler_params(("parallel", "parallel")),
        name="in_projection",
    )(x, mod_l, norm_g, wa, wd, wz, wab, wg)


def _att_kernel(q_ref, k_ref, v_ref, bias_ref, o_ref, s_ref, *, n_blocks):
    j = pl.program_id(1)
    rows = n_blocks * ATT_ROWS
    base = jnp.clip(j - 1, 0, n_blocks - 3)
    lane = lax.broadcasted_iota(jnp.int32, (GRID_W, LANES), 1)
    low = lane < NA_DH

    def rows_body(it, carry):
        probs = []
        for u in range(ATT_UNROLL):
            ri = it * ATT_UNROLL + u
            r = j * ATT_ROWS + ri
            sr = jnp.clip(r - NA_KH // 2, 0, rows - NA_KH)
            delta = r - sr
            off = pl.multiple_of((sr - base * ATT_ROWS) * GRID_W, GRID_W)
            qrow = pl.multiple_of(ri * GRID_W, GRID_W)
            for p in range(NA_HEADS // 2):
                lanes = slice(p * LANES, (p + 1) * LANES)
                qp = q_ref[pl.ds(qrow, GRID_W), lanes]
                zero = jnp.zeros_like(qp)
                lhs = jnp.concatenate([jnp.where(low, qp, zero), jnp.where(low, zero, qp)], axis=0)
                kp = k_ref[0, pl.ds(off, NA_KH * GRID_W), lanes]
                s_ref[len(probs)] = lax.dot_general(lhs, kp, (((1,), (1,)), ((), ())),
                                                    preferred_element_type=F32)
                probs.append((qrow, off, delta, p, lanes, len(probs)))
        pvs = []
        for qrow, off, delta, p, lanes, slot in probs:
            s = s_ref[slot] + bias_ref[delta, p]
            m = jnp.max(s, axis=-1, keepdims=True)
            e = jnp.exp(s - m)
            den = jnp.sum(e, axis=-1, keepdims=True)
            vp = v_ref[0, pl.ds(off, NA_KH * GRID_W), lanes]
            pvs.append((jnp.dot(e.astype(BF16), vp, preferred_element_type=F32), den))
        for (qrow, off, delta, p, lanes, slot), (pv, den) in zip(probs, pvs):
            pv = pv / den
            o_ref[pl.ds(qrow, GRID_W), lanes] = jnp.where(low, pv[0:GRID_W], pv[GRID_W:]).astype(BF16)
        return carry

    lax.fori_loop(0, ATT_ROWS // ATT_UNROLL, rows_body, 0)


def _attention(qa, ka, va, bias):
    nb, seq, _ = qa.shape
    blk = ATT_ROWS * GRID_W
    n_blocks = seq // blk
    assert n_blocks >= 3
    win = ATT_WIN_ROWS * GRID_W

    def win_map(b, j):
        return (b, jnp.clip(j - 1, 0, n_blocks - 3) * blk, 0)

    win_spec = pl.BlockSpec((pl.Element(1), pl.Element(win), pl.Element(NA_W)), win_map)
    return pl.pallas_call(
        functools.partial(_att_kernel, n_blocks=n_blocks),
        grid=(nb, n_blocks),
        in_specs=[
            pl.BlockSpec((None, blk, NA_W), lambda b, j: (b, j, 0)),
            win_spec, win_spec,
            _resident(bias.shape, lambda b, j: (0, 0, 0, 0)),
        ],
        out_specs=pl.BlockSpec((None, blk, NA_W), lambda b, j: (b, j, 0)),
        out_shape=jax.ShapeDtypeStruct((nb, seq, NA_W), BF16),
        scratch_shapes=[pltpu.VMEM((ATT_UNROLL * NA_HEADS // 2, 2 * GRID_W, NA_KH * GRID_W), F32)],
        compiler_params=_compiler_params(("parallel", "parallel")),
        name="neighborhood_attention",
    )(qa, ka, va, bias)


def _attention_bias(rpb):
    qc = np.arange(GRID_W)
    kc = np.arange(GRID_W)
    start = np.clip(qc - NA_KW // 2, 0, GRID_W - NA_KW)
    valid = (kc[None, :] >= start[:, None]) & (kc[None, :] < start[:, None] + NA_KW)
    neg = jnp.asarray(np.where(valid, 0.0, -1e30).astype(np.float32))
    dc = np.clip(kc[None, :] - qc[:, None] + NA_KW - 1, 0, 2 * NA_KW - 2)
    delta = np.arange(NA_KH)
    dr = np.arange(NA_KH)[None, :] - delta[:, None] + (NA_KH - 1)
    t = rpb.astype(F32)[:, dr][:, :, :, dc]
    t = t + neg[None, None, None]
    t = t.transpose(1, 0, 3, 2, 4).reshape(NA_KH, NA_HEADS // 2, 2 * GRID_W, NA_KH * GRID_W)
    return t


def _prep_kernel(x_ref, prev_ref, next_ref, ab_ref, cw_ref, alog_ref, dtb_ref,
                 q_ref, k_ref, v_ref, gb_ref, pad_ref, *, tm):
    i = pl.program_id(1)
    n = pl.num_programs(1)
    pad_ref[0:HALO, :] = jnp.where(i > 0, prev_ref[...], 0.0)
    pad_ref[HALO:HALO + tm, :] = x_ref[...]
    pad_ref[HALO + tm:, :] = jnp.where(i < n - 1, next_ref[...], 0.0)
    first = HALO - CONV_W // 2
    acc = pad_ref[first:first + tm, :] * cw_ref[0:1, :]
    for t in range(1, CONV_W):
        acc = acc + pad_ref[first + t:first + t + tm, :] * cw_ref[t:t + 1, :]
    y = _silu(acc)
    for h in range(DN_HEADS):
        ql = slice(h * DN_DK, (h + 1) * DN_DK)
        qh = y[:, ql]
        qn = qh * lax.rsqrt(jnp.sum(qh * qh, axis=-1, keepdims=True) + EPS)
        q_ref[:, ql] = qn * (DN_DK ** -0.5)
        kh = y[:, DN_QK_W + h * DN_DK:DN_QK_W + (h + 1) * DN_DK]
        k_ref[:, ql] = kh * lax.rsqrt(jnp.sum(kh * kh, axis=-1, keepdims=True) + EPS)
    v_ref[...] = y[:, 2 * DN_QK_W:]
    ab = ab_ref[...]
    beta = _sigmoid(ab)
    a = ab + dtb_ref[...]
    softplus = jnp.maximum(a, 0.0) + jnp.log1p(jnp.exp(-jnp.abs(a)))
    g = -jnp.exp(alog_ref[...]) * softplus
    col = lax.broadcasted_iota(jnp.int32, ab.shape, 1)
    raw = jnp.where(col < 2 * DN_HEADS, beta, jnp.where(col < AB_W, g, 0.0))
    r = lax.broadcasted_iota(jnp.int32, (CHUNK, CHUNK), 0)
    t = lax.broadcasted_iota(jnp.int32, (CHUNK, CHUNK), 1)
    sums = jnp.concatenate([(t <= r).astype(F32), (t >= r).astype(F32),
                            jnp.ones((CHUNK, CHUNK), F32)], axis=0)
    ccol = lax.broadcasted_iota(jnp.int32, (CHUNK, LANES), 1)
    for c in range(tm // CHUNK):
        rows = slice(c * CHUNK, (c + 1) * CHUNK)
        res = jnp.dot(sums, raw[rows], precision=HIGHEST, preferred_element_type=F32)
        prefix = pltpu.roll(res[0:CHUNK], GB_CUM - GB_G, axis=1)
        suffix = pltpu.roll(res[CHUNK:2 * CHUNK], GB_CUM - GB_G, axis=1)
        total = pltpu.roll(res[2 * CHUNK:], GB_TOT - GB_G, axis=1)
        cum = jnp.where(ccol < GB_CUM + DN_HEADS, prefix, suffix)
        gb_ref[rows, :] = jnp.where(ccol < GB_CUM, raw[rows],
                                    jnp.where(ccol < GB_TOT, cum,
                                              jnp.where(ccol < GB_END, total, 0.0)))


def _dn_prep(dqkv, ab, conv_w, a_log, dt_bias):
    nb, seq, _ = dqkv.shape
    tm = min(TM_PREP, seq)
    per = tm // HALO
    n_halo = seq // HALO
    pad_cols = lambda t: jnp.pad(t.reshape(1, 2 * DN_HEADS).astype(F32),
                                 ((0, 0), (2 * DN_HEADS, LANES - AB_W)))
    tok = lambda width: pl.BlockSpec((None, tm, width), lambda b, i: (b, i, 0))
    return pl.pallas_call(
        functools.partial(_prep_kernel, tm=tm),
        grid=(nb, seq // tm),
        in_specs=[
            tok(DN_QKV_W),
            pl.BlockSpec((None, HALO, DN_QKV_W), lambda b, i: (b, jnp.maximum(i * per - 1, 0), 0)),
            pl.BlockSpec((None, HALO, DN_QKV_W),
                         lambda b, i: (b, jnp.minimum((i + 1) * per, n_halo - 1), 0)),
            tok(LANES),
            _resident((CONV_W, DN_QKV_W), lambda b, i: (0, 0)),
            _resident((1, LANES), lambda b, i: (0, 0)),
            _resident((1, LANES), lambda b, i: (0, 0)),
        ],
        out_specs=(tok(DN_QK_W), tok(DN_QK_W), tok(DN_W), tok(LANES)),
        out_shape=(
            jax.ShapeDtypeStruct((nb, seq, DN_QK_W), F32),
            jax.ShapeDtypeStruct((nb, seq, DN_QK_W), F32),
            jax.ShapeDtypeStruct((nb, seq, DN_W), F32),
            jax.ShapeDtypeStruct((nb, seq, LANES), F32),
        ),
        scratch_shapes=[pltpu.VMEM((tm + 2 * HALO, DN_QKV_W), F32)],
        compiler_params=_compiler_params(("parallel", "parallel")),
        name="deltanet_prep",
    )(dqkv, dqkv, dqkv, ab, conv_w, pad_cols(a_log), pad_cols(dt_bias))


_N_SOLVE_LEVELS = 6


def _dn_kernel(qf_ref, kf_ref, vf_ref, gf_ref, qb_ref, kb_ref, vb_ref, gbb_ref,
               of_ref, ob_ref, sf_ref, sb_ref):
    @pl.when(pl.program_id(1) == 0)
    def _():
        sf_ref[...] = jnp.zeros_like(sf_ref)
        sb_ref[...] = jnp.zeros_like(sb_ref)

    row = lax.broadcasted_iota(jnp.int32, (CHUNK, CHUNK), 0)
    colm = lax.broadcasted_iota(jnp.int32, (CHUNK, CHUNK), 1)
    nt = (((1,), (1,)), ((), ()))
    tn = (((0,), (0,)), ((), ()))

    probs = []
    for reverse in (False, True):
        q_ref, k_ref, v_ref, g_ref, o_ref, s_ref = (
            (qb_ref, kb_ref, vb_ref, gbb_ref, ob_ref, sb_ref) if reverse
            else (qf_ref, kf_ref, vf_ref, gf_ref, of_ref, sf_ref))
        incl, strict = (colm >= row, colm > row) if reverse else (colm <= row, colm < row)
        for c in range(DN_GROUP):
            rows = slice(c * CHUNK, (c + 1) * CHUNK)
            gb = g_ref[rows, :]
            gb_t = gb.T
            for h in range(DN_HEADS):
                lanes = slice(h * DN_DK, (h + 1) * DN_DK)
                hd = h + (DN_HEADS if reverse else 0)
                q, k, v = q_ref[rows, lanes], k_ref[rows, lanes], v_ref[rows, lanes]
                q_b, k_b = q.astype(BF16), k.astype(BF16)
                probs.append(dict(
                    reverse=reverse, c=c, h=h, rows=rows, lanes=lanes, o_ref=o_ref, s_ref=s_ref,
                    incl=incl, strict=strict, q=q, k=k, v=v,
                    beta=gb[:, GB_BETA + hd:GB_BETA + hd + 1],
                    g_col=gb[:, GB_CUM + hd:GB_CUM + hd + 1],
                    g_row=gb_t[GB_CUM + hd:GB_CUM + hd + 1, :],
                    g_tot=gb[:, GB_TOT + hd:GB_TOT + hd + 1],
                    qk_kk=lax.dot_general(jnp.concatenate([q_b, k_b], axis=0), k_b, nt,
                                          preferred_element_type=F32)))

    for p in probs:
        decay = jnp.exp(jnp.where(p["incl"], p["g_col"] - p["g_row"], -jnp.inf))
        p["qk"] = (p["qk_kk"][:CHUNK] * decay).astype(BF16)
        p["a"] = jnp.where(p["strict"], p["beta"] * p["qk_kk"][CHUNK:] * decay, 0.0)
        exp_g = jnp.exp(p["g_col"])
        p["x"] = jnp.concatenate([(p["beta"] * exp_g) * p["k"], p["beta"] * p["v"]], axis=1)
        p["q_dec"] = p["q"] * exp_g
        p["k_dec"] = (p["k"] * jnp.exp(p["g_tot"] - p["g_col"])).astype(BF16)
        p["g_last"] = jnp.exp(jnp.concatenate([p["g_tot"], p["g_tot"]], axis=0))

    for level in range(_N_SOLVE_LEVELS):
        last = level + 1 == _N_SOLVE_LEVELS
        prods = []
        for p in probs:
            rhs = p["x"] if last else jnp.concatenate([p["x"], p["a"]], axis=1)
            prods.append(jnp.dot(p["a"].astype(BF16), rhs.astype(BF16),
                                 preferred_element_type=F32))
        for p, prod in zip(probs, prods):
            ax = prod[:, :DN_DK + DN_DV]
            p["x"] = p["x"] - ax if level == 0 else p["x"] + ax
            if not last:
                p["a"] = prod[:, DN_DK + DN_DV:]
    for p in probs:
        p["w_q"] = jnp.concatenate([p["x"][:, :DN_DK], p["q_dec"]], axis=0).astype(BF16)
        p["u"] = p["x"][:, DN_DK:]

    for step in range(DN_GROUP):
        chains = [p for p in probs
                  if p["c"] == (DN_GROUP - 1 - step if p["reverse"] else step)]
        states = [p["s_ref"][p["h"]] for p in chains]
        ws_qs = [jnp.dot(p["w_q"], s.astype(BF16), preferred_element_type=F32)
                 for p, s in zip(chains, states)]
        v_new = [(p["u"] - r[:CHUNK]).astype(BF16) for p, r in zip(chains, ws_qs)]
        intra = [jnp.dot(p["qk"], vn, preferred_element_type=F32) for p, vn in zip(chains, v_new)]
        ds = [lax.dot_general(p["k_dec"], vn, tn, preferred_element_type=F32)
              for p, vn in zip(chains, v_new)]
        for p, s, r, oi, d in zip(chains, states, ws_qs, intra, ds):
            p["o_ref"][p["rows"], p["lanes"]] = r[CHUNK:] + oi
            p["s_ref"][p["h"]] = s * p["g_last"] + d


def _delta_rule(qn, kn, v, gb):
    nb, seq, _ = qn.shape
    tg = DN_GROUP * CHUNK
    n = seq // tg
    fwd = lambda width: pl.BlockSpec((None, tg, width), lambda b, i: (b, i, 0))
    bwd = lambda width: pl.BlockSpec((None, tg, width), lambda b, i: (b, n - 1 - i, 0))
    state = pltpu.VMEM((DN_HEADS, DN_DK, DN_DV), F32)
    return pl.pallas_call(
        _dn_kernel,
        grid=(nb, n),
        in_specs=[fwd(DN_QK_W), fwd(DN_QK_W), fwd(DN_W), fwd(LANES),
                  bwd(DN_QK_W), bwd(DN_QK_W), bwd(DN_W), bwd(LANES)],
        out_specs=(fwd(DN_W), bwd(DN_W)),
        out_shape=(jax.ShapeDtypeStruct((nb, seq, DN_W), F32),
                   jax.ShapeDtypeStruct((nb, seq, DN_W), F32)),
        scratch_shapes=[state, state],
        compiler_params=_compiler_params(("parallel", "arbitrary")),
        name="delta_rule",
    )(qn, kn, v, gb, qn, kn, v, gb)


def _post_kernel(x_ref, oa_ref, of_ref, ob_ref, z_ref, gate_ref, mod_ref, dng_ref, g2_ref,
                 fg_ref, wba_ref, wbd_ref, wo_ref, w1_ref, w2_ref, y_ref, *, final):
    od = of_ref[...] + ob_ref[...]
    z = z_ref[...]
    dng = dng_ref[...]
    parts = []
    for h in range(DN_HEADS):
        lanes = slice(h * DN_DV, (h + 1) * DN_DV)
        parts.append(_rms(od[:, lanes], dng) * _silu(z[:, lanes]))
    od_n = jnp.concatenate(parts, axis=1).astype(BF16)
    br_a = jnp.dot(oa_ref[...], wba_ref[...], preferred_element_type=F32)
    br_d = jnp.dot(od_n, wbd_ref[...], preferred_element_type=F32)
    merged = _sigmoid(gate_ref[:, 0:D_MODEL]) * br_a + _sigmoid(gate_ref[:, D_MODEL:]) * br_d
    mix = jnp.dot(merged.astype(BF16), wo_ref[...], preferred_element_type=F32)
    gt1 = mod_ref[:, 2 * D_MODEL:3 * D_MODEL]
    sh2 = mod_ref[:, 3 * D_MODEL:4 * D_MODEL]
    sc2 = mod_ref[:, 4 * D_MODEL:5 * D_MODEL]
    gt2 = mod_ref[:, 5 * D_MODEL:6 * D_MODEL]
    x = x_ref[...] + gt1 * mix
    h2 = (_rms(x, g2_ref[...]) * (1.0 + sc2) + sh2).astype(BF16)
    acc = jnp.zeros_like(x)
    for j in range(D_FF // D_MODEL):
        cols = slice(j * D_MODEL, (j + 1) * D_MODEL)
        hid = jnp.maximum(jnp.dot(h2, w1_ref[:, cols], preferred_element_type=F32), 0.0)
        acc = acc + jnp.dot((hid * hid).astype(BF16), w2_ref[cols, :], preferred_element_type=F32)
    x = x + gt2 * acc
    if final:
        x = _rms(x, fg_ref[...])
    y_ref[...] = x


def _post(x, oa, of, ob, z, gates, mod_l, dn_norm_g, norm_mlp_g, final_g, weights, final):
    nb, seq, _ = x.shape
    tm = TM_POST
    tok = lambda width: pl.BlockSpec((None, tm, width), lambda b, i: (b, i, 0))
    vec = lambda width: _resident((1, width), lambda b, i: (0, 0))
    full = lambda w: _resident(w.shape, lambda b, i: (0, 0))
    return pl.pallas_call(
        functools.partial(_post_kernel, final=final),
        grid=(nb, seq // tm),
        in_specs=[
            tok(D_MODEL), tok(NA_W), tok(DN_W), tok(DN_W), tok(DN_W), tok(N_GATE),
            pl.BlockSpec((None, 1, N_MOD * D_MODEL), lambda b, i: (b, 0, 0)),
            vec(DN_DV), vec(D_MODEL), vec(D_MODEL),
        ] + [full(w) for w in weights],
        out_specs=tok(D_MODEL),
        out_shape=jax.ShapeDtypeStruct((nb, seq, D_MODEL), F32),
        compiler_params=_compiler_params(("parallel", "parallel")),
        name="merge_mlp",
    )(x, oa, of, ob, z, gates, mod_l, dn_norm_g, norm_mlp_g, final_g, *weights)


def _split_w_in(w_in_l):
    o1 = 3 * NA_W
    o2 = o1 + DN_QKV_W
    o3 = o2 + DN_W
    o4 = o3 + AB_W
    wab = jnp.pad(w_in_l[:, o3:o4], ((0, 0), (0, LANES - AB_W)))
    parts = (w_in_l[:, :o1], w_in_l[:, o1:o2], w_in_l[:, o2:o3], wab, w_in_l[:, o4:])
    return tuple(p.astype(BF16) for p in parts)


def _trunk(x, mod, p):
    for l in range(DEPTH):
        qa, ka, va, dqkv, z, ab, gates = _in_projection(x, mod[l], p["norm_mix_g"][l], p["w_in"][l])
        oa = _attention(qa, ka, va, p["bias"][l])
        qn, kn, v, gb = _dn_prep(dqkv, ab, p["dn_conv"][l], p["dn_a_log"][l], p["dn_dt_bias"][l])
        of, ob = _delta_rule(qn, kn, v, gb)
        x = _post(x, oa, of, ob, z, gates, mod[l], p["dn_norm_g"][l], p["norm_mlp_g"][l],
                  p["final_norm_g"], p["post_w"][l], final=(l == DEPTH - 1))
    return x


def kernel(x_prompt, x_sample, c_prompt, c_sample, norm_mix_g, norm_mlp_g, w_ada, b_ada, w_in,
           na_rpb, dn_conv, dn_a_log, dn_dt_bias, dn_norm_g, w_br_attn, w_br_dn, w_out, w_mlp1,
           w_mlp2, final_norm_g):
    row = lambda t: t.reshape(1, -1).astype(F32)
    p = {
        "norm_mix_g": [row(norm_mix_g[l]) for l in range(DEPTH)],
        "norm_mlp_g": [row(norm_mlp_g[l]) for l in range(DEPTH)],
        "dn_norm_g": [row(dn_norm_g[l]) for l in range(DEPTH)],
        "final_norm_g": row(final_norm_g),
        "w_in": [_split_w_in(w_in[l]) for l in range(DEPTH)],
        "bias": [_attention_bias(na_rpb[l]) for l in range(DEPTH)],
        "dn_conv": [dn_conv[l].astype(F32) for l in range(DEPTH)],
        "dn_a_log": dn_a_log,
        "dn_dt_bias": dn_dt_bias,
        "post_w": [tuple(w[l].astype(BF16) for w in (w_br_attn, w_br_dn, w_out, w_mlp1, w_mlp2))
                   for l in range(DEPTH)],
    }
    n_prompt = c_prompt.shape[0]
    mod = _modulation(jnp.concatenate([c_prompt, c_sample], axis=0), w_ada, b_ada)
    mod = mod.reshape(DEPTH, -1, 1, N_MOD * D_MODEL)
    y_prompt = _trunk(x_prompt, mod[:, :n_prompt], p)
    y_sample = _trunk(x_sample, mod[:, n_prompt:], p)
    return (y_prompt, y_sample)
```

```python
import functools

import jax
import jax.numpy as jnp
import numpy as np
from jax import lax
from jax.experimental import pallas as pl
from jax.experimental.pallas import tpu as pltpu

D_MODEL = 1024
DEPTH = 2
GRID_W = 64
NA_HEADS = 8
NA_DH = 64
NA_W = NA_HEADS * NA_DH
NA_KH = 8
NA_KW = 16
DN_HEADS = 4
DN_DK = 128
DN_DV = 128
DN_QK_W = DN_HEADS * DN_DK
DN_W = DN_HEADS * DN_DV
DN_QKV_W = 2 * DN_QK_W + DN_W
CONV_W = 5
CHUNK = 64
D_FF = 4 * D_MODEL
N_MOD = 6
EPS = 1e-6
N_GATE = 2 * D_MODEL
AB_W = 4 * DN_HEADS
GB_BETA = 0
GB_G = 2 * DN_HEADS
GB_CUM = 4 * DN_HEADS
GB_TOT = 6 * DN_HEADS
GB_END = 8 * DN_HEADS

LANES = 128
SUBLANES = 8
MXU_N = 256
VMEM_LIMIT_BYTES = 56 * 1024 * 1024

TM_IN = 512
TM_POST = 512
ATT_ROWS = 8
ATT_WIN_ROWS = 3 * ATT_ROWS
ATT_UNROLL = 4
DN_GROUP = 4
HALO = SUBLANES

BF16 = jnp.bfloat16
F32 = jnp.float32
HIGHEST = lax.Precision.HIGHEST


def _compiler_params(semantics):
    return pltpu.CompilerParams(dimension_semantics=semantics,
                                vmem_limit_bytes=VMEM_LIMIT_BYTES)


def _resident(block_shape, index_map):
    return pl.BlockSpec(block_shape, index_map, pipeline_mode=pl.Buffered(1))


def _silu(x):
    return x * (1.0 / (1.0 + jnp.exp(-x)))


def _sigmoid(x):
    return 1.0 / (1.0 + jnp.exp(-x))


def _rms(x, g):
    return (x * lax.rsqrt(jnp.mean(x * x, axis=-1, keepdims=True) + EPS)) * g


def _mod_kernel(c_ref, w_ref, b_ref, o_ref):
    c = _silu(c_ref[...])
    o_ref[...] = jnp.dot(c, w_ref[...], precision=HIGHEST,
                         preferred_element_type=F32) + b_ref[...]


def _modulation(c, w_ada, b_ada):
    nb = c.shape[0]
    tn = 1024
    n_out = N_MOD * D_MODEL
    return pl.pallas_call(
        _mod_kernel,
        grid=(DEPTH, n_out // tn),
        in_specs=[
            pl.BlockSpec((nb, D_MODEL), lambda l, j: (0, 0)),
            pl.BlockSpec((None, D_MODEL, tn), lambda l, j: (l, 0, j)),
            pl.BlockSpec((None, 1, tn), lambda l, j: (l, 0, j)),
        ],
        out_specs=pl.BlockSpec((None, nb, tn), lambda l, j: (l, 0, j)),
        out_shape=jax.ShapeDtypeStruct((DEPTH, nb, n_out), F32),
        compiler_params=_compiler_params(("arbitrary", "arbitrary")),
        name="adaln_mod",
    )(c, w_ada, b_ada.reshape(DEPTH, 1, n_out))


def _chunk_scan(x, reverse):
    tm = x.shape[0]
    pos = lax.broadcasted_iota(jnp.int32, x.shape, 0) & (CHUNK - 1)
    step = 1
    while step < CHUNK:
        if reverse:
            shifted = pltpu.roll(x, tm - step, axis=0)
            x = x + jnp.where(pos < CHUNK - step, shifted, 0.0)
        else:
            shifted = pltpu.roll(x, step, axis=0)
            x = x + jnp.where(pos >= step, shifted, 0.0)
        step *= 2
    return x


def _in_kernel(x_ref, xprev_ref, xnext_ref, mod_ref, g_ref, wd_ref, wa_ref, wz_ref, wab_ref,
               wg_ref, cw_ref, alog_ref, dtb_ref,
               qa_ref, ka_ref, va_ref, qn_ref, kn_ref, v_ref, gb_ref, z_ref, gate_ref,
               pad_ref, h_ref, *, tm):
    i = pl.program_id(1)
    n = pl.num_programs(1)
    sh1 = mod_ref[:, 0:D_MODEL]
    sc1 = mod_ref[:, D_MODEL:2 * D_MODEL]
    gain = g_ref[...]
    normed = lambda xv: (_rms(xv, gain) * (1.0 + sc1) + sh1).astype(BF16)
    h_ref[...] = normed(x_ref[...])
    h_halo = normed(jnp.concatenate([xprev_ref[...], xnext_ref[...]], axis=0))

    def deltanet_block(c):
        cols = slice(c * MXU_N, (c + 1) * MXU_N)
        w = wd_ref[:, cols]
        d_halo = jnp.dot(h_halo, w, preferred_element_type=F32)
        pad_ref[0:HALO, cols] = jnp.where(i > 0, d_halo[0:HALO], 0.0)
        pad_ref[HALO + tm:, cols] = jnp.where(i < n - 1, d_halo[HALO:], 0.0)
        pad_ref[HALO:HALO + tm, cols] = jnp.dot(h_ref[...],w, preferred_element_type=F32)
        first = HALO - CONV_W // 2
        acc = pad_ref[first:first + tm, cols] * cw_ref[0:1, cols]
        for t in range(1, CONV_W):
            acc = acc + pad_ref[first + t:first + t + tm, cols] * cw_ref[t:t + 1, cols]
        y = _silu(acc)
        out_ref, scale = ((qn_ref, DN_DK ** -0.5), (kn_ref, None), (v_ref, None))[c * MXU_N // DN_QK_W]
        base = c * MXU_N % DN_QK_W
        for hd in range(MXU_N // DN_DK):
            yh = y[:, hd * DN_DK:(hd + 1) * DN_DK]
            if out_ref is not v_ref:
                yh = yh * lax.rsqrt(jnp.sum(yh * yh, axis=-1, keepdims=True) + EPS)
            if scale is not None:
                yh = yh * scale
            out_ref[:, base + hd * DN_DK:base + (hd + 1) * DN_DK] = yh

    def plain_block(w_ref, out_ref, c, scale=None):
        cols = slice(c * MXU_N, (c + 1) * MXU_N)
        r = jnp.dot(h_ref[...],w_ref[:, cols], preferred_element_type=F32)
        out_ref[:, cols] = (r if scale is None else r * scale).astype(BF16)

    others = [functools.partial(plain_block, wa_ref, qa_ref, c, NA_DH ** -0.5)
              for c in range(NA_W // MXU_N)]
    others += [functools.partial(plain_block, w_ref, out_ref, c)
               for w_ref, out_ref, width in ((wa_ref.at[:, NA_W:2 * NA_W], ka_ref, NA_W),
                                             (wa_ref.at[:, 2 * NA_W:], va_ref, NA_W),
                                             (wz_ref, z_ref, DN_W), (wg_ref, gate_ref, N_GATE))
               for c in range(width // MXU_N)]
    ab = jnp.dot(h_ref[...],wab_ref[...], preferred_element_type=F32)
    n_dn = DN_QKV_W // MXU_N
    per_dn = -(-len(others) // n_dn)
    for c in range(n_dn):
        deltanet_block(c)
        for emit in others[c * per_dn:(c + 1) * per_dn]:
            emit()

    beta = _sigmoid(ab)
    a = ab + dtb_ref[...]
    softplus = jnp.maximum(a, 0.0) + jnp.log1p(jnp.exp(-jnp.abs(a)))
    g = -jnp.exp(alog_ref[...]) * softplus
    col = lax.broadcasted_iota(jnp.int32, ab.shape, 1)
    raw = jnp.where(col < GB_G, beta, jnp.where(col < GB_CUM, g, 0.0))
    prefix = _chunk_scan(raw, reverse=False)
    suffix = _chunk_scan(raw, reverse=True)
    total = pltpu.roll(prefix + suffix - raw, GB_TOT - GB_G, axis=1)
    cum = jnp.where(col < GB_CUM + DN_HEADS, pltpu.roll(prefix, GB_CUM - GB_G, axis=1),
                    pltpu.roll(suffix, GB_CUM - GB_G, axis=1))
    gb_ref[...] = jnp.where(col < GB_CUM, raw,
                            jnp.where(col < GB_TOT, cum, jnp.where(col < GB_END, total, 0.0)))


def _in_projection(x, mod_l, norm_g, w_parts, conv_w, a_log, dt_bias):
    nb, seq, _ = x.shape
    tm = TM_IN
    per = tm // HALO
    n_halo = seq // HALO
    wa, wd, wz, wab, wg = w_parts
    pad_cols = lambda t: jnp.pad(t.reshape(1, 2 * DN_HEADS).astype(F32),
                                 ((0, 0), (GB_G, LANES - GB_CUM)))
    tok = lambda width: pl.BlockSpec((None, tm, width), lambda b, i: (b, i, 0))
    full = lambda w: _resident(w.shape, lambda b, i: (0, 0))
    vec = lambda width: _resident((1, width), lambda b, i: (0, 0))
    out_shape = (
        jax.ShapeDtypeStruct((nb, seq, NA_W), BF16),
        jax.ShapeDtypeStruct((nb, seq, NA_W), BF16),
        jax.ShapeDtypeStruct((nb, seq, NA_W), BF16),
        jax.ShapeDtypeStruct((nb, seq, DN_QK_W), F32),
        jax.ShapeDtypeStruct((nb, seq, DN_QK_W), F32),
        jax.ShapeDtypeStruct((nb, seq, DN_W), F32),
        jax.ShapeDtypeStruct((nb, seq, LANES), F32),
        jax.ShapeDtypeStruct((nb, seq, DN_W), BF16),
        jax.ShapeDtypeStruct((nb, seq, N_GATE), BF16),
    )
    return pl.pallas_call(
        functools.partial(_in_kernel, tm=tm),
        grid=(nb, seq // tm),
        in_specs=[
            tok(D_MODEL),
            pl.BlockSpec((None, HALO, D_MODEL), lambda b, i: (b, jnp.maximum(i * per - 1, 0), 0)),
            pl.BlockSpec((None, HALO, D_MODEL),
                         lambda b, i: (b, jnp.minimum((i + 1) * per, n_halo - 1), 0)),
            pl.BlockSpec((None, 1, N_MOD * D_MODEL), lambda b, i: (b, 0, 0)),
            vec(D_MODEL),
            full(wd), full(wa), full(wz), full(wab), full(wg),
            _resident((CONV_W, DN_QKV_W), lambda b, i: (0, 0)),
            vec(LANES), vec(LANES),
        ],
        out_specs=(tok(NA_W), tok(NA_W), tok(NA_W), tok(DN_QK_W), tok(DN_QK_W), tok(DN_W),
                   tok(LANES), tok(DN_W), tok(N_GATE)),
        out_shape=out_shape,
        scratch_shapes=[pltpu.VMEM((tm + 2 * HALO, DN_QKV_W), F32),
                        pltpu.VMEM((tm, D_MODEL), BF16)],
        compiler_params=_compiler_params(("parallel", "parallel")),
        name="in_projection",
    )(x, x, x, mod_l, norm_g, wd, wa, wz, wab, wg, conv_w, pad_cols(a_log), pad_cols(dt_bias))


def _att_kernel(q_ref, k_ref, v_ref, bias_ref, o_ref, s_ref, *, n_blocks):
    j = pl.program_id(1)
    rows = n_blocks * ATT_ROWS
    base = jnp.clip(j - 1, 0, n_blocks - 3)
    lane = lax.broadcasted_iota(jnp.int32, (GRID_W, LANES), 1)
    low = lane < NA_DH

    def rows_body(it, carry):
        probs = []
        for u in range(ATT_UNROLL):
            ri = it * ATT_UNROLL + u
            r = j * ATT_ROWS + ri
            sr = jnp.clip(r - NA_KH // 2, 0, rows - NA_KH)
            delta = r - sr
            off = pl.multiple_of((sr - base * ATT_ROWS) * GRID_W, GRID_W)
            qrow = pl.multiple_of(ri * GRID_W, GRID_W)
            for p in range(NA_HEADS // 2):
                lanes = slice(p * LANES, (p + 1) * LANES)
                qp = q_ref[pl.ds(qrow, GRID_W), lanes]
                zero = jnp.zeros_like(qp)
                lhs = jnp.concatenate([jnp.where(low, qp, zero), jnp.where(low, zero, qp)], axis=0)
                kp = k_ref[0, pl.ds(off, NA_KH * GRID_W), lanes]
                s_ref[len(probs)] = lax.dot_general(lhs, kp, (((1,), (1,)), ((), ())),
                                                    preferred_element_type=F32)
                probs.append((qrow, off, delta, p, lanes, len(probs)))
        pvs = []
        for qrow, off, delta, p, lanes, slot in probs:
            s = s_ref[slot] + bias_ref[delta, p]
            m = jnp.max(s, axis=-1, keepdims=True)
            e = jnp.exp(s - m)
            den = jnp.sum(e, axis=-1, keepdims=True)
            vp = v_ref[0, pl.ds(off, NA_KH * GRID_W), lanes]
            pvs.append((jnp.dot(e.astype(BF16), vp, preferred_element_type=F32), den))
        for (qrow, off, delta, p, lanes, slot), (pv, den) in zip(probs, pvs):
            pv = pv / den
            o_ref[pl.ds(qrow, GRID_W), lanes] = jnp.where(low, pv[0:GRID_W], pv[GRID_W:]).astype(BF16)
        return carry

    lax.fori_loop(0, ATT_ROWS // ATT_UNROLL, rows_body, 0)


def _attention(qa, ka, va, bias):
    nb, seq, _ = qa.shape
    blk = ATT_ROWS * GRID_W
    n_blocks = seq // blk
    assert n_blocks >= 3
    win = ATT_WIN_ROWS * GRID_W

    def win_map(b, j):
        return (b, jnp.clip(j - 1, 0, n_blocks - 3) * blk, 0)

    win_spec = pl.BlockSpec((pl.Element(1), pl.Element(win), pl.Element(NA_W)), win_map)
    return pl.pallas_call(
        functools.partial(_att_kernel, n_blocks=n_blocks),
        grid=(nb, n_blocks),
        in_specs=[
            pl.BlockSpec((None, blk, NA_W), lambda b, j: (b, j, 0)),
            win_spec, win_spec,
            _resident(bias.shape, lambda b, j: (0, 0, 0, 0)),
        ],
        out_specs=pl.BlockSpec((None, blk, NA_W), lambda b, j: (b, j, 0)),
        out_shape=jax.ShapeDtypeStruct((nb, seq, NA_W), BF16),
        scratch_shapes=[pltpu.VMEM((ATT_UNROLL * NA_HEADS // 2, 2 * GRID_W, NA_KH * GRID_W), F32)],
        compiler_params=_compiler_params(("parallel", "parallel")),
        name="neighborhood_attention",
    )(qa, ka, va, bias)


def _attention_bias(rpb):
    qc = np.arange(GRID_W)
    kc = np.arange(GRID_W)
    start = np.clip(qc - NA_KW // 2, 0, GRID_W - NA_KW)
    valid = (kc[None, :] >= start[:, None]) & (kc[None, :] < start[:, None] + NA_KW)
    neg = jnp.asarray(np.where(valid, 0.0, -1e30).astype(np.float32))
    dc = np.clip(kc[None, :] - qc[:, None] + NA_KW - 1, 0, 2 * NA_KW - 2)
    delta = np.arange(NA_KH)
    dr = np.arange(NA_KH)[None, :] - delta[:, None] + (NA_KH - 1)
    t = rpb.astype(F32)[:, dr][:, :, :, dc]
    t = t + neg[None, None, None]
    t = t.transpose(1, 0, 3, 2, 4).reshape(NA_KH, NA_HEADS // 2, 2 * GRID_W, NA_KH * GRID_W)
    return t


_LOG2_CHUNK = CHUNK.bit_length() - 1


def _dn_prepare(refs, stage, slot):
    row = lax.broadcasted_iota(jnp.int32, (CHUNK, CHUNK), 0)
    colm = lax.broadcasted_iota(jnp.int32, (CHUNK, CHUNK), 1)
    nt = (((1,), (1,)), ((), ()))
    probs = []
    for reverse in (False, True):
        q_ref, k_ref, v_ref, g_ref = refs[reverse]
        incl, strict = (colm >= row, colm > row) if reverse else (colm <= row, colm < row)
        for c in range(DN_GROUP):
            rows = slice(c * CHUNK, (c + 1) * CHUNK)
            gb = g_ref[rows, :]
            gb_t = gb.T
            for h in range(DN_HEADS):
                lanes = slice(h * DN_DK, (h + 1) * DN_DK)
                hd = h + (DN_HEADS if reverse else 0)
                q, k, v = q_ref[rows, lanes], k_ref[rows, lanes], v_ref[rows, lanes]
                q_b, k_b = q.astype(BF16), k.astype(BF16)
                probs.append(dict(
                    reverse=reverse, incl=incl, strict=strict, q=q, k=k, v=v,
                    beta=gb[:, GB_BETA + hd:GB_BETA + hd + 1],
                    g_col=gb[:, GB_CUM + hd:GB_CUM + hd + 1],
                    g_row=gb_t[GB_CUM + hd:GB_CUM + hd + 1, :],
                    g_tot_row=gb_t[GB_TOT + hd:GB_TOT + hd + 1, :],
                    qk_kk=lax.dot_general(jnp.concatenate([q_b, k_b], axis=0), k_b, nt,
                                          preferred_element_type=F32)))
    yield
    for i, p in enumerate(probs):
        beta = jnp.broadcast_to(p["beta"], (CHUNK, DN_DK))
        g_col = jnp.broadcast_to(p["g_col"], (CHUNK, DN_DK))
        g_tot = jnp.broadcast_to(p["g_tot_row"], (CHUNK, CHUNK))
        g_tot = jnp.concatenate([g_tot, g_tot], axis=1)
        decay = jnp.exp(jnp.where(p["incl"], g_col[:, :CHUNK] - p["g_row"], -jnp.inf))
        stage["qk"][slot, i] = (p["qk_kk"][:CHUNK] * decay).astype(BF16)
        p["a"] = jnp.where(p["strict"], beta[:, :CHUNK] * p["qk_kk"][CHUNK:] * decay, 0.0)
        exp_g = jnp.exp(g_col)
        p["x"] = jnp.concatenate([(beta * exp_g) * p["k"], beta * p["v"]], axis=1)
        p["q_dec"] = p["q"] * exp_g
        k_dec = p["k"] * jnp.exp(g_tot - g_col)
        stage["kdt"][slot, i] = k_dec.T.astype(BF16)
        stage["gl"][slot, i] = jnp.exp(g_tot[0:SUBLANES])
    wide = DN_HEADS * CHUNK
    diag = ((lax.broadcasted_iota(jnp.int32, (wide, wide), 0) >> _LOG2_CHUNK)
            == (lax.broadcasted_iota(jnp.int32, (wide, wide), 1) >> _LOG2_CHUNK))

    def block_diag(m):
        m = jnp.concatenate([m] * DN_HEADS, axis=0)
        return jnp.where(diag, m, 0.0).astype(BF16)

    ri = lax.broadcasted_iota(jnp.int32, (CHUNK, wide), 0)
    ci = lax.broadcasted_iota(jnp.int32, (CHUNK, wide), 1) & (CHUNK - 1)
    groups = []
    for g0 in range(0, len(probs), DN_HEADS):
        members = probs[g0:g0 + DN_HEADS]
        groups.append(dict(members=members, first=g0, reverse=members[0]["reverse"],
                           a=jnp.concatenate([p["a"] for p in members], axis=1)))
    for level in range(_LOG2_CHUNK):
        for g in groups:
            lo_i, lo_j = ((ri >> level) & 1) == 1, ((ci >> level) & 1) == 0
            couple = ((ri >> (level + 1)) == (ci >> (level + 1))) & (
                (~lo_i & ~lo_j) if g["reverse"] else (lo_i & lo_j))
            g["l"] = jnp.where(couple, g["a"], 0.0)
        if level == 0:
            for g in groups:
                g["e"] = -g["l"]
            continue
        prods = [jnp.dot(g["l"].astype(BF16), block_diag(g["e"]), preferred_element_type=F32)
                 for g in groups]
        yield
        for g, r in zip(groups, prods):
            g["m"] = g["l"] + r
        prods = [jnp.dot(g["e"].astype(BF16), block_diag(g["m"]), preferred_element_type=F32)
                 for g in groups]
        yield
        for g, r in zip(groups, prods):
            g["e"] = g["e"] - (g["m"] + r)
    low = lax.broadcasted_iota(jnp.int32, (CHUNK, 2 * CHUNK), 1) < CHUNK
    prods = []
    for g in groups:
        for pair in range(DN_HEADS // 2):
            e = g["e"][:, pair * 2 * CHUNK:(pair + 1) * 2 * CHUNK]
            lhs = jnp.concatenate([jnp.where(low, e, 0.0), jnp.where(low, 0.0, e)], axis=0)
            rhs = jnp.concatenate([p["x"] for p in g["members"][2 * pair:2 * pair + 2]], axis=0)
            prods.append(jnp.dot(lhs.astype(BF16), rhs.astype(BF16), preferred_element_type=F32))
    yield
    for i, p in enumerate(probs):
        r = prods[i // 2]
        x = p["x"] + (r[CHUNK:] if i % 2 else r[:CHUNK])
        stage["wq"][slot, i] = jnp.concatenate([x[:, :DN_DK], p["q_dec"]], axis=0).astype(BF16)
        stage["u"][slot, i] = x[:, DN_DK:]


def _dn_scan(out_refs, state_refs, stage, slot):
    states = {(rev, h): state_refs[rev][h] for rev in (False, True) for h in range(DN_HEADS)}
    for step in range(DN_GROUP):
        chains = []
        for reverse in (False, True):
            c = DN_GROUP - 1 - step if reverse else step
            for h in range(DN_HEADS):
                chains.append((reverse, c, h, ((DN_GROUP if reverse else 0) + c) * DN_HEADS + h))
        ws_qs = [jnp.dot(stage["wq"][slot, i], states[rev, h].astype(BF16),
                         preferred_element_type=F32) for rev, c, h, i in chains]
        yield
        v_new = [(stage["u"][slot, i] - r[:CHUNK]).astype(BF16)
                 for (rev, c, h, i), r in zip(chains, ws_qs)]
        intra = [jnp.dot(stage["qk"][slot, i], vn, preferred_element_type=F32)
                 for (rev, c, h, i), vn in zip(chains, v_new)]
        ds = [jnp.dot(stage["kdt"][slot, i], vn, preferred_element_type=F32)
              for (rev, c, h, i), vn in zip(chains, v_new)]
        yield
        for (rev, c, h, i), r, oi, d in zip(chains, ws_qs, intra, ds):
            out_refs[rev][c * CHUNK:(c + 1) * CHUNK, h * DN_DV:(h + 1) * DN_DV] = r[CHUNK:] + oi
            g_last = jnp.concatenate([stage["gl"][slot, i]] * (DN_DK // SUBLANES), axis=0)
            states[rev, h] = states[rev, h] * g_last + d
    for (rev, h), s in states.items():
        state_refs[rev][h] = s


def _dn_kernel(qf_ref, kf_ref, vf_ref, gf_ref, qb_ref, kb_ref, vb_ref, gbb_ref,
               of_ref, ob_ref, sf_ref, sb_ref, wq_ref, u_ref, qk_ref, kdt_ref, gl_ref, *, n):
    t = pl.program_id(0)
    stage = dict(wq=wq_ref, u=u_ref, qk=qk_ref, kdt=kdt_ref, gl=gl_ref)

    @pl.when(t == 0)
    def _():
        for ref in stage.values():
            ref[...] = jnp.zeros_like(ref)

    @pl.when((t == 0) | (lax.rem(t + n - 1, n) == 0))
    def _():
        sf_ref[...] = jnp.zeros_like(sf_ref)
        sb_ref[...] = jnp.zeros_like(sb_ref)

    slot = lax.rem(t, 2)
    prepare = _dn_prepare({False: (qf_ref, kf_ref, vf_ref, gf_ref),
                           True: (qb_ref, kb_ref, vb_ref, gbb_ref)}, stage, slot)
    scan = _dn_scan({False: of_ref, True: ob_ref}, {False: sf_ref, True: sb_ref}, stage, 1 - slot)
    live = [prepare, scan]
    while live:
        for gen in list(live):
            if next(gen, StopIteration) is StopIteration:
                live.remove(gen)


def _delta_rule(qn, kn, v, gb):
    nb, seq, _ = qn.shape
    tg = DN_GROUP * CHUNK
    n = seq // tg
    total = nb * n
    n_prob = 2 * DN_GROUP * DN_HEADS

    def in_map(reverse):
        def index(t):
            tt = jnp.minimum(t, total - 1)
            i = lax.rem(tt, n)
            return (tt // n, n - 1 - i if reverse else i, 0)
        return index

    def out_map(reverse):
        def index(t):
            tt = jnp.maximum(t - 1, 0)
            i = lax.rem(tt, n)
            return (tt // n, n - 1 - i if reverse else i, 0)
        return index

    spec = lambda width, index: pl.BlockSpec((None, tg, width), index)
    widths = (DN_QK_W, DN_QK_W, DN_W, LANES)
    state = pltpu.VMEM((DN_HEADS, DN_DK, DN_DV), F32)
    return pl.pallas_call(
        functools.partial(_dn_kernel, n=n),
        grid=(total + 1,),
        in_specs=[spec(w, in_map(False)) for w in widths] + [spec(w, in_map(True)) for w in widths],
        out_specs=(spec(DN_W, out_map(False)), spec(DN_W, out_map(True))),
        out_shape=(jax.ShapeDtypeStruct((nb, seq, DN_W), F32),
                   jax.ShapeDtypeStruct((nb, seq, DN_W), F32)),
        scratch_shapes=[
            state, state,
            pltpu.VMEM((2, n_prob, DN_DK, DN_DK), BF16),
            pltpu.VMEM((2, n_prob, CHUNK, DN_DV), F32),
            pltpu.VMEM((2, n_prob, CHUNK, CHUNK), BF16),
            pltpu.VMEM((2, n_prob, DN_DK, CHUNK), BF16),
            pltpu.VMEM((2, n_prob, SUBLANES, LANES), F32),
        ],
        compiler_params=_compiler_params(("arbitrary",)),
        name="delta_rule",
    )(qn, kn, v, gb, qn, kn, v, gb)


def _post_kernel(x_ref, oa_ref, of_ref, ob_ref, z_ref, gate_ref, mod_ref, dng_ref, g2_ref,
                 fg_ref, wba_ref, wbd_ref, wo_ref, w1_ref, w2_ref, y_ref, *, final):
    od = of_ref[...] + ob_ref[...]
    z = z_ref[...].astype(F32)
    dng = dng_ref[...]
    parts = []
    for h in range(DN_HEADS):
        lanes = slice(h * DN_DV, (h + 1) * DN_DV)
        parts.append(_rms(od[:, lanes], dng) * _silu(z[:, lanes]))
    od_n = jnp.concatenate(parts, axis=1).astype(BF16)
    br_a = jnp.dot(oa_ref[...], wba_ref[...], preferred_element_type=F32)
    br_d = jnp.dot(od_n, wbd_ref[...], preferred_element_type=F32)
    gate_a = _sigmoid(gate_ref[:, 0:D_MODEL].astype(F32))
    gate_d = _sigmoid(gate_ref[:, D_MODEL:].astype(F32))
    merged = gate_a * br_a + gate_d * br_d
    mix = jnp.dot(merged.astype(BF16), wo_ref[...], preferred_element_type=F32)
    gt1 = mod_ref[:, 2 * D_MODEL:3 * D_MODEL]
    sh2 = mod_ref[:, 3 * D_MODEL:4 * D_MODEL]
    sc2 = mod_ref[:, 4 * D_MODEL:5 * D_MODEL]
    gt2 = mod_ref[:, 5 * D_MODEL:6 * D_MODEL]
    x = x_ref[...] + gt1 * mix
    h2 = (_rms(x, g2_ref[...]) * (1.0 + sc2) + sh2).astype(BF16)
    acc = jnp.zeros_like(x)
    for j in range(D_FF // D_MODEL):
        cols = slice(j * D_MODEL, (j + 1) * D_MODEL)
        hid = jnp.maximum(jnp.dot(h2, w1_ref[:, cols], preferred_element_type=F32), 0.0)
        acc = acc + jnp.dot((hid * hid).astype(BF16), w2_ref[cols, :], preferred_element_type=F32)
    x = x + gt2 * acc
    if final:
        x = _rms(x, fg_ref[...])
    y_ref[...] = x


def _post(x, oa, of, ob, z, gates, mod_l, dn_norm_g, norm_mlp_g, final_g, weights, final):
    nb, seq, _ = x.shape
    tm = TM_POST
    tok = lambda width: pl.BlockSpec((None, tm, width), lambda b, i: (b, i, 0))
    vec = lambda width: _resident((1, width), lambda b, i: (0, 0))
    full = lambda w: _resident(w.shape, lambda b, i: (0, 0))
    return pl.pallas_call(
        functools.partial(_post_kernel, final=final),
        grid=(nb, seq // tm),
        in_specs=[
            tok(D_MODEL), tok(NA_W), tok(DN_W), tok(DN_W), tok(DN_W), tok(N_GATE),
            pl.BlockSpec((None, 1, N_MOD * D_MODEL), lambda b, i: (b, 0, 0)),
            vec(DN_DV), vec(D_MODEL), vec(D_MODEL),
        ] + [full(w) for w in weights],
        out_specs=tok(D_MODEL),
        out_shape=jax.ShapeDtypeStruct((nb, seq, D_MODEL), F32),
        compiler_params=_compiler_params(("parallel", "parallel")),
        name="merge_mlp",
    )(x, oa, of, ob, z, gates, mod_l, dn_norm_g, norm_mlp_g, final_g, *weights)


def _split_w_in(w_in_l):
    o1 = 3 * NA_W
    o2 = o1 + DN_QKV_W
    o3 = o2 + DN_W
    o4 = o3 + AB_W
    wab = jnp.pad(w_in_l[:, o3:o4], ((0, 0), (0, LANES - AB_W)))
    parts = (w_in_l[:, :o1], w_in_l[:, o1:o2], w_in_l[:, o2:o3], wab, w_in_l[:, o4:])
    return tuple(p.astype(BF16) for p in parts)


def _trunk(x, mod, p):
    for l in range(DEPTH):
        qa, ka, va, qn, kn, v, gb, z, gates = _in_projection(
            x, mod[l], p["norm_mix_g"][l], p["w_in"][l], p["dn_conv"][l], p["dn_a_log"][l],
            p["dn_dt_bias"][l])
        oa = _attention(qa, ka, va, p["bias"][l])
        of, ob = _delta_rule(qn, kn, v, gb)
        x = _post(x, oa, of, ob, z, gates, mod[l], p["dn_norm_g"][l], p["norm_mlp_g"][l],
                  p["final_norm_g"], p["post_w"][l], final=(l == DEPTH - 1))
    return x


def kernel(x_prompt, x_sample, c_prompt, c_sample, norm_mix_g, norm_mlp_g, w_ada, b_ada, w_in,
           na_rpb, dn_conv, dn_a_log, dn_dt_bias, dn_norm_g, w_br_attn, w_br_dn, w_out, w_mlp1,
           w_mlp2, final_norm_g):
    row = lambda t: t.reshape(1, -1).astype(F32)
    p = {
        "norm_mix_g": [row(norm_mix_g[l]) for l in range(DEPTH)],
        "norm_mlp_g": [row(norm_mlp_g[l]) for l in range(DEPTH)],
        "dn_norm_g": [row(dn_norm_g[l]) for l in range(DEPTH)],
        "final_norm_g": row(final_norm_g),
        "w_in": [_split_w_in(w_in[l]) for l in range(DEPTH)],
        "bias": [_attention_bias(na_rpb[l]) for l in range(DEPTH)],
        "dn_conv": [dn_conv[l].astype(F32) for l in range(DEPTH)],
        "dn_a_log": dn_a_log,
        "dn_dt_bias": dn_dt_bias,
        "post_w": [tuple(w[l].astype(BF16) for w in (w_br_attn, w_br_dn, w_out, w_mlp1, w_mlp2))
                   for l in range(DEPTH)],
    }
    n_prompt = c_prompt.shape[0]
    mod = _modulation(jnp.concatenate([c_prompt, c_sample], axis=0), w_ada, b_ada)
    mod = mod.reshape(DEPTH, -1, 1, N_MOD * D_MODEL)
    y_prompt = _trunk(x_prompt, mod[:, :n_prompt], p)
    y_sample = _trunk(x_sample, mod[:, n_prompt:], p)
    return (y_prompt, y_sample)
```

```python
import functools

import jax
import jax.numpy as jnp
import numpy as np
from jax import lax
from jax.experimental import pallas as pl
from jax.experimental.pallas import tpu as pltpu

D_MODEL = 1024
DEPTH = 2
GRID_W = 64
NA_HEADS = 8
NA_DH = 64
NA_W = NA_HEADS * NA_DH
NA_KH = 8
NA_KW = 16
DN_HEADS = 4
DN_DK = 128
DN_DV = 128
DN_QK_W = DN_HEADS * DN_DK
DN_W = DN_HEADS * DN_DV
DN_QKV_W = 2 * DN_QK_W + DN_W
CONV_W = 5
CHUNK = 64
D_FF = 4 * D_MODEL
N_MOD = 6
EPS = 1e-6
N_GATE = 2 * D_MODEL
AB_W = 4 * DN_HEADS
GB_BETA = 0
GB_G = 2 * DN_HEADS
GB_CUM = 4 * DN_HEADS
GB_TOT = 6 * DN_HEADS
GB_END = 8 * DN_HEADS

LANES = 128
SUBLANES = 8
MXU_N = 256
VMEM_LIMIT_BYTES = 56 * 1024 * 1024

TM_IN = 512
TM_POST = 512
ATT_ROWS = 8
ATT_WIN_ROWS = 3 * ATT_ROWS
ATT_UNROLL = 4
DN_GROUP = 8
HALO = SUBLANES

BF16 = jnp.bfloat16
F32 = jnp.float32
HIGHEST = lax.Precision.HIGHEST


def _compiler_params(semantics):
    return pltpu.CompilerParams(dimension_semantics=semantics,
                                vmem_limit_bytes=VMEM_LIMIT_BYTES)


def _resident(block_shape, index_map):
    return pl.BlockSpec(block_shape, index_map, pipeline_mode=pl.Buffered(1))


def _silu(x):
    half = 0.5 * x
    return half * (jnp.tanh(half) + 1.0)


def _sigmoid(x):
    return 0.5 * (jnp.tanh(0.5 * x) + 1.0)


def _rms(x, g):
    return (x * lax.rsqrt(jnp.mean(x * x, axis=-1, keepdims=True) + EPS)) * g


def _mod_kernel(c_ref, w_ref, b_ref, o_ref):
    c = _silu(c_ref[...])
    o_ref[...] = jnp.dot(c, w_ref[...], precision=HIGHEST,
                         preferred_element_type=F32) + b_ref[...]


def _modulation(c, w_ada, b_ada):
    nb = c.shape[0]
    tn = 1024
    n_out = N_MOD * D_MODEL
    return pl.pallas_call(
        _mod_kernel,
        grid=(DEPTH, n_out // tn),
        in_specs=[
            pl.BlockSpec((nb, D_MODEL), lambda l, j: (0, 0)),
            pl.BlockSpec((None, D_MODEL, tn), lambda l, j: (l, 0, j)),
            pl.BlockSpec((None, 1, tn), lambda l, j: (l, 0, j)),
        ],
        out_specs=pl.BlockSpec((None, nb, tn), lambda l, j: (l, 0, j)),
        out_shape=jax.ShapeDtypeStruct((DEPTH, nb, n_out), F32),
        compiler_params=_compiler_params(("arbitrary", "arbitrary")),
        name="adaln_mod",
    )(c, w_ada, b_ada.reshape(DEPTH, 1, n_out))


def _chunk_scan(x, reverse):
    tm = x.shape[0]
    pos = lax.broadcasted_iota(jnp.int32, x.shape, 0) & (CHUNK - 1)
    step = 1
    while step < CHUNK:
        if reverse:
            shifted = pltpu.roll(x, tm - step, axis=0)
            x = x + jnp.where(pos < CHUNK - step, shifted, 0.0)
        else:
            shifted = pltpu.roll(x, step, axis=0)
            x = x + jnp.where(pos >= step, shifted, 0.0)
        step *= 2
    return x


def _in_kernel(x_ref, xprev_ref, xnext_ref, mod_ref, g_ref, wd_ref, wa_ref, wz_ref, wab_ref,
               wg_ref, cw_ref, alog_ref, dtb_ref,
               qa_ref, ka_ref, va_ref, qn_ref, kn_ref, v_ref, gb_ref, z_ref, gate_ref,
               pad_ref, h_ref, *, tm):
    i = pl.program_id(1)
    n = pl.num_programs(1)
    sh1 = mod_ref[:, 0:D_MODEL]
    sc1 = mod_ref[:, D_MODEL:2 * D_MODEL]
    gain = g_ref[...]
    normed = lambda xv: (_rms(xv, gain) * (1.0 + sc1) + sh1).astype(BF16)
    h_ref[...] = normed(x_ref[...])
    h_halo = normed(jnp.concatenate([xprev_ref[...], xnext_ref[...]], axis=0))

    def deltanet_block(c):
        cols = slice(c * MXU_N, (c + 1) * MXU_N)
        w = wd_ref[:, cols]
        d_halo = jnp.dot(h_halo, w, preferred_element_type=F32)
        pad_ref[0:HALO, cols] = jnp.where(i > 0, d_halo[0:HALO], 0.0)
        pad_ref[HALO + tm:, cols] = jnp.where(i < n - 1, d_halo[HALO:], 0.0)
        pad_ref[HALO:HALO + tm, cols] = jnp.dot(h_ref[...], w, preferred_element_type=F32)
        first = HALO - CONV_W // 2
        acc = pad_ref[first:first + tm, cols] * cw_ref[0:1, cols]
        for t in range(1, CONV_W):
            acc = acc + pad_ref[first + t:first + t + tm, cols] * cw_ref[t:t + 1, cols]
        y = _silu(acc)
        out_ref, scale = ((qn_ref, DN_DK ** -0.5), (kn_ref, None), (v_ref, None))[c * MXU_N // DN_QK_W]
        base = c * MXU_N % DN_QK_W
        for hd in range(MXU_N // DN_DK):
            yh = y[:, hd * DN_DK:(hd + 1) * DN_DK]
            if out_ref is not v_ref:
                yh = yh * lax.rsqrt(jnp.sum(yh * yh, axis=-1, keepdims=True) + EPS)
            if scale is not None:
                yh = yh * scale
            out_ref[:, base + hd * DN_DK:base + (hd + 1) * DN_DK] = yh

    def plain_block(w_ref, out_ref, c):
        cols = slice(c * MXU_N, (c + 1) * MXU_N)
        r = jnp.dot(h_ref[...], w_ref[:, cols], preferred_element_type=F32)
        out_ref[:, cols] = r.astype(BF16)

    others = [functools.partial(plain_block, w_ref, out_ref, c)
              for w_ref, out_ref, width in ((wa_ref, qa_ref, NA_W),
                                            (wa_ref.at[:, NA_W:2 * NA_W], ka_ref, NA_W),
                                            (wa_ref.at[:, 2 * NA_W:], va_ref, NA_W),
                                            (wz_ref, z_ref, DN_W), (wg_ref, gate_ref, N_GATE))
              for c in range(width // MXU_N)]
    ab = jnp.dot(h_ref[...], wab_ref[...], preferred_element_type=F32)
    n_dn = DN_QKV_W // MXU_N
    per_dn = 1
    for c in range(n_dn):
        deltanet_block(c)
        for emit in others[c * per_dn:(c + 1) * per_dn]:
            emit()
    for emit in others[n_dn * per_dn:]:
        emit()

    beta = _sigmoid(ab)
    a = ab + dtb_ref[...]
    softplus = jnp.maximum(a, 0.0) + jnp.log1p(jnp.exp(-jnp.abs(a)))
    g = -jnp.exp(alog_ref[...]) * softplus
    col = lax.broadcasted_iota(jnp.int32, ab.shape, 1)
    raw = jnp.where(col < GB_G, beta, jnp.where(col < GB_CUM, g, 0.0))
    prefix = _chunk_scan(raw, reverse=False)
    suffix = _chunk_scan(raw, reverse=True)
    total = pltpu.roll(prefix + suffix - raw, GB_TOT - GB_G, axis=1)
    cum = jnp.where(col < GB_CUM + DN_HEADS, pltpu.roll(prefix, GB_CUM - GB_G, axis=1),
                    pltpu.roll(suffix, GB_CUM - GB_G, axis=1))
    gb_ref[...] = jnp.where(col < GB_CUM, raw,
                            jnp.where(col < GB_TOT, cum, jnp.where(col < GB_END, total, 0.0)))


def _in_projection(x, mod_l, norm_g, w_parts, conv_w, a_log, dt_bias):
    nb, seq, _ = x.shape
    tm = TM_IN
    per = tm // HALO
    n_halo = seq // HALO
    wa, wd, wz, wab, wg = w_parts
    pad_cols = lambda t: jnp.pad(t.reshape(1, 2 * DN_HEADS).astype(F32),
                                 ((0, 0), (GB_G, LANES - GB_CUM)))
    tok = lambda width: pl.BlockSpec((None, tm, width), lambda b, i: (b, i, 0))
    full = lambda w: _resident(w.shape, lambda b, i: (0, 0))
    vec = lambda width: _resident((1, width), lambda b, i: (0, 0))
    out_shape = (
        jax.ShapeDtypeStruct((nb, seq, NA_W), BF16),
        jax.ShapeDtypeStruct((nb, seq, NA_W), BF16),
        jax.ShapeDtypeStruct((nb, seq, NA_W), BF16),
        jax.ShapeDtypeStruct((nb, seq, DN_QK_W), F32),
        jax.ShapeDtypeStruct((nb, seq, DN_QK_W), F32),
        jax.ShapeDtypeStruct((nb, seq, DN_W), F32),
        jax.ShapeDtypeStruct((nb, seq, LANES), F32),
        jax.ShapeDtypeStruct((nb, seq, DN_W), BF16),
        jax.ShapeDtypeStruct((nb, seq, N_GATE), BF16),
    )
    return pl.pallas_call(
        functools.partial(_in_kernel, tm=tm),
        grid=(nb, seq // tm),
        in_specs=[
            tok(D_MODEL),
            pl.BlockSpec((None, HALO, D_MODEL), lambda b, i: (b, jnp.maximum(i * per - 1, 0), 0)),
            pl.BlockSpec((None, HALO, D_MODEL),
                         lambda b, i: (b, jnp.minimum((i + 1) * per, n_halo - 1), 0)),
            pl.BlockSpec((None, 1, N_MOD * D_MODEL), lambda b, i: (b, 0, 0)),
            vec(D_MODEL),
            full(wd), full(wa), full(wz), full(wab), full(wg),
            _resident((CONV_W, DN_QKV_W), lambda b, i: (0, 0)),
            vec(LANES), vec(LANES),
        ],
        out_specs=(tok(NA_W), tok(NA_W), tok(NA_W), tok(DN_QK_W), tok(DN_QK_W), tok(DN_W),
                   tok(LANES), tok(DN_W), tok(N_GATE)),
        out_shape=out_shape,
        scratch_shapes=[pltpu.VMEM((tm + 2 * HALO, DN_QKV_W), F32),
                        pltpu.VMEM((tm, D_MODEL), BF16)],
        compiler_params=_compiler_params(("parallel", "parallel")),
        name="in_projection",
    )(x, x, x, mod_l, norm_g, wd, wa, wz, wab, wg, conv_w, pad_cols(a_log), pad_cols(dt_bias))


def _att_kernel(q_ref, k_ref, v_ref, bias_ref, o_ref, s_ref, *, n_blocks):
    j = pl.program_id(1)
    rows = n_blocks * ATT_ROWS
    base = jnp.clip(j - 1, 0, n_blocks - 3)
    lane = lax.broadcasted_iota(jnp.int32, (GRID_W, LANES), 1)
    low = lane < NA_DH

    def rows_body(it, carry):
        probs = []
        for u in range(ATT_UNROLL):
            ri = it * ATT_UNROLL + u
            r = j * ATT_ROWS + ri
            sr = jnp.clip(r - NA_KH // 2, 0, rows - NA_KH)
            delta = r - sr
            off = pl.multiple_of((sr - base * ATT_ROWS) * GRID_W, GRID_W)
            qrow = pl.multiple_of(ri * GRID_W, GRID_W)
            for p in range(NA_HEADS // 2):
                lanes = slice(p * LANES, (p + 1) * LANES)
                qp = q_ref[pl.ds(qrow, GRID_W), lanes]
                zero = jnp.zeros_like(qp)
                lhs = jnp.concatenate([jnp.where(low, qp, zero), jnp.where(low, zero, qp)], axis=0)
                kp = k_ref[0, pl.ds(off, NA_KH * GRID_W), lanes]
                s_ref[len(probs)] = lax.dot_general(lhs, kp, (((1,), (1,)), ((), ())),
                                                    preferred_element_type=F32)
                probs.append((qrow, off, delta, p, lanes, len(probs)))
        pvs = []
        for qrow, off, delta, p, lanes, slot in probs:
            s = s_ref[slot] + bias_ref[delta, p]
            m = jnp.max(s, axis=-1, keepdims=True)
            e = jnp.exp(s - m)
            den = jnp.sum(e, axis=-1, keepdims=True)
            vp = v_ref[0, pl.ds(off, NA_KH * GRID_W), lanes]
            pvs.append((jnp.dot(e.astype(BF16), vp, preferred_element_type=F32), den))
        for (qrow, off, delta, p, lanes, slot), (pv, den) in zip(probs, pvs):
            pv = pv / den
            o_ref[pl.ds(qrow, GRID_W), lanes] = jnp.where(low, pv[0:GRID_W], pv[GRID_W:]).astype(BF16)
        return carry

    lax.fori_loop(0, ATT_ROWS // ATT_UNROLL, rows_body, 0)


def _attention(qa, ka, va, bias):
    nb, seq, _ = qa.shape
    blk = ATT_ROWS * GRID_W
    n_blocks = seq // blk
    assert n_blocks >= 3
    win = ATT_WIN_ROWS * GRID_W

    def win_map(b, j):
        return (b, jnp.clip(j - 1, 0, n_blocks - 3) * blk, 0)

    win_spec = pl.BlockSpec((pl.Element(1), pl.Element(win), pl.Element(NA_W)), win_map)
    return pl.pallas_call(
        functools.partial(_att_kernel, n_blocks=n_blocks),
        grid=(nb, n_blocks),
        in_specs=[
            pl.BlockSpec((None, blk, NA_W), lambda b, j: (b, j, 0)),
            win_spec, win_spec,
            _resident(bias.shape, lambda b, j: (0, 0, 0, 0)),
        ],
        out_specs=pl.BlockSpec((None, blk, NA_W), lambda b, j: (b, j, 0)),
        out_shape=jax.ShapeDtypeStruct((nb, seq, NA_W), BF16),
        scratch_shapes=[pltpu.VMEM((ATT_UNROLL * NA_HEADS // 2, 2 * GRID_W, NA_KH * GRID_W), F32)],
        compiler_params=_compiler_params(("parallel", "parallel")),
        name="neighborhood_attention",
    )(qa, ka, va, bias)


def _attention_bias(rpb):
    qc = np.arange(GRID_W)
    kc = np.arange(GRID_W)
    start = np.clip(qc - NA_KW // 2, 0, GRID_W - NA_KW)
    valid = (kc[None, :] >= start[:, None]) & (kc[None, :] < start[:, None] + NA_KW)
    neg = jnp.asarray(np.where(valid, 0.0, -1e30).astype(np.float32))
    dc = np.clip(kc[None, :] - qc[:, None] + NA_KW - 1, 0, 2 * NA_KW - 2)
    delta = np.arange(NA_KH)
    dr = np.arange(NA_KH)[None, :] - delta[:, None] + (NA_KH - 1)
    onehot = jnp.asarray((dc[None] == np.arange(2 * NA_KW - 1)[:, None, None]).astype(np.float32))
    t = jnp.einsum('hdic,cqk->dhqik', rpb.astype(F32)[:, dr], onehot, precision=HIGHEST)
    t = t + neg[None, None, :, None, :]
    return t.reshape(NA_KH, NA_HEADS // 2, 2 * GRID_W, NA_KH * GRID_W)


_LOG2_CHUNK = CHUNK.bit_length() - 1


def _dn_prepare(refs, stage, slot):
    row = lax.broadcasted_iota(jnp.int32, (CHUNK, CHUNK), 0)
    colm = lax.broadcasted_iota(jnp.int32, (CHUNK, CHUNK), 1)
    nt = (((1,), (1,)), ((), ()))
    probs = []
    for reverse in (False, True):
        q_ref, k_ref, v_ref, g_ref = refs[reverse]
        incl, strict = (colm >= row, colm > row) if reverse else (colm <= row, colm < row)
        for c in range(DN_GROUP):
            rows = slice(c * CHUNK, (c + 1) * CHUNK)
            gb = g_ref[rows, :]
            gb_t = gb.T
            for h in range(DN_HEADS):
                lanes = slice(h * DN_DK, (h + 1) * DN_DK)
                hd = h + (DN_HEADS if reverse else 0)
                q, k, v = q_ref[rows, lanes], k_ref[rows, lanes], v_ref[rows, lanes]
                q_b, k_b = q.astype(BF16), k.astype(BF16)
                probs.append(dict(
                    reverse=reverse, incl=incl, strict=strict, q=q, k=k, v=v,
                    beta=gb[:, GB_BETA + hd:GB_BETA + hd + 1],
                    g_col=gb[:, GB_CUM + hd:GB_CUM + hd + 1],
                    g_row=gb_t[GB_CUM + hd:GB_CUM + hd + 1, :],
                    g_tot_row=gb_t[GB_TOT + hd:GB_TOT + hd + 1, :],
                    qk_kk=lax.dot_general(jnp.concatenate([q_b, k_b], axis=0), k_b, nt,
                                          preferred_element_type=F32)))
    yield
    for i, p in enumerate(probs):
        beta = jnp.broadcast_to(p["beta"], (CHUNK, DN_DK))
        g_col = jnp.broadcast_to(p["g_col"], (CHUNK, DN_DK))
        g_tot = jnp.broadcast_to(p["g_tot_row"], (CHUNK, CHUNK))
        g_tot = jnp.concatenate([g_tot, g_tot], axis=1)
        decay = jnp.exp(jnp.where(p["incl"], g_col[:, :CHUNK] - p["g_row"], -jnp.inf))
        stage["qk"][slot, i] = (p["qk_kk"][:CHUNK] * decay).astype(BF16)
        p["a"] = jnp.where(p["strict"], beta[:, :CHUNK] * p["qk_kk"][CHUNK:] * decay, 0.0)
        exp_g = jnp.exp(g_col)
        p["x"] = jnp.concatenate([(beta * exp_g) * p["k"], beta * p["v"]], axis=1)
        p["q_dec"] = p["q"] * exp_g
        k_dec = p["k"] * jnp.exp(g_tot - g_col)
        stage["kdt"][slot, i] = k_dec.T.astype(BF16)
        stage["gl"][slot, i] = jnp.exp(g_tot[0:SUBLANES])
    wide = DN_HEADS * CHUNK
    diag = ((lax.broadcasted_iota(jnp.int32, (wide, wide), 0) >> _LOG2_CHUNK)
            == (lax.broadcasted_iota(jnp.int32, (wide, wide), 1) >> _LOG2_CHUNK))

    def block_diag(m):
        m = jnp.concatenate([m] * DN_HEADS, axis=0)
        return jnp.where(diag, m, 0.0).astype(BF16)

    ri = lax.broadcasted_iota(jnp.int32, (CHUNK, wide), 0)
    ci = lax.broadcasted_iota(jnp.int32, (CHUNK, wide), 1) & (CHUNK - 1)
    groups = []
    for g0 in range(0, len(probs), DN_HEADS):
        members = probs[g0:g0 + DN_HEADS]
        groups.append(dict(members=members, first=g0, reverse=members[0]["reverse"],
                           a=jnp.concatenate([p["a"] for p in members], axis=1)))
    for level in range(_LOG2_CHUNK):
        for g in groups:
            lo_i, lo_j = ((ri >> level) & 1) == 1, ((ci >> level) & 1) == 0
            couple = ((ri >> (level + 1)) == (ci >> (level + 1))) & (
                (~lo_i & ~lo_j) if g["reverse"] else (lo_i & lo_j))
            g["l"] = jnp.where(couple, g["a"], 0.0)
        if level == 0:
            for g in groups:
                g["e"] = -g["l"]
            continue
        prods = [jnp.dot(g["l"].astype(BF16), block_diag(g["e"]), preferred_element_type=F32)
                 for g in groups]
        yield
        for g, r in zip(groups, prods):
            g["m"] = g["l"] + r
        prods = [jnp.dot(g["e"].astype(BF16), block_diag(g["m"]), preferred_element_type=F32)
                 for g in groups]
        yield
        for g, r in zip(groups, prods):
            g["e"] = g["e"] - (g["m"] + r)
    low = lax.broadcasted_iota(jnp.int32, (CHUNK, 2 * CHUNK), 1) < CHUNK
    prods = []
    for g in groups:
        for pair in range(DN_HEADS // 2):
            e = g["e"][:, pair * 2 * CHUNK:(pair + 1) * 2 * CHUNK]
            lhs = jnp.concatenate([jnp.where(low, e, 0.0), jnp.where(low, 0.0, e)], axis=0)
            rhs = jnp.concatenate([p["x"] for p in g["members"][2 * pair:2 * pair + 2]], axis=0)
            prods.append(jnp.dot(lhs.astype(BF16), rhs.astype(BF16), preferred_element_type=F32))
    yield
    for i, p in enumerate(probs):
        r = prods[i // 2]
        x = p["x"] + (r[CHUNK:] if i % 2 else r[:CHUNK])
        stage["wq"][slot, i] = jnp.concatenate([x[:, :DN_DK], p["q_dec"]], axis=0).astype(BF16)
        stage["u"][slot, i] = x[:, DN_DK:]


def _dn_scan(out_refs, state_refs, stage, slot):
    states = {(rev, h): state_refs[rev][h] for rev in (False, True) for h in range(DN_HEADS)}
    for step in range(DN_GROUP):
        chains = []
        for reverse in (False, True):
            c = DN_GROUP - 1 - step if reverse else step
            for h in range(DN_HEADS):
                chains.append((reverse, c, h, ((DN_GROUP if reverse else 0) + c) * DN_HEADS + h))
        ws_qs = [jnp.dot(stage["wq"][slot, i], states[rev, h].astype(BF16),
                         preferred_element_type=F32) for rev, c, h, i in chains]
        yield
        v_new = [(stage["u"][slot, i] - r[:CHUNK]).astype(BF16)
                 for (rev, c, h, i), r in zip(chains, ws_qs)]
        intra = [jnp.dot(stage["qk"][slot, i], vn, preferred_element_type=F32)
                 for (rev, c, h, i), vn in zip(chains, v_new)]
        ds = [jnp.dot(stage["kdt"][slot, i], vn, preferred_element_type=F32)
              for (rev, c, h, i), vn in zip(chains, v_new)]
        yield
        for (rev, c, h, i), r, oi, d in zip(chains, ws_qs, intra, ds):
            out_refs[rev][c * CHUNK:(c + 1) * CHUNK, h * DN_DV:(h + 1) * DN_DV] = r[CHUNK:] + oi
            g_last = jnp.concatenate([stage["gl"][slot, i]] * (DN_DK // SUBLANES), axis=0)
            states[rev, h] = states[rev, h] * g_last + d
    for (rev, h), s in states.items():
        state_refs[rev][h] = s


def _dn_kernel(qf_ref, kf_ref, vf_ref, gf_ref, qb_ref, kb_ref, vb_ref, gbb_ref,
               of_ref, ob_ref, sf_ref, sb_ref, wq_ref, u_ref, qk_ref, kdt_ref, gl_ref, *, n):
    t = pl.program_id(0)
    stage = dict(wq=wq_ref, u=u_ref, qk=qk_ref, kdt=kdt_ref, gl=gl_ref)

    @pl.when(t == 0)
    def _():
        for ref in stage.values():
            ref[...] = jnp.zeros_like(ref)

    @pl.when((t == 0) | (lax.rem(t + n - 1, n) == 0))
    def _():
        sf_ref[...] = jnp.zeros_like(sf_ref)
        sb_ref[...] = jnp.zeros_like(sb_ref)

    slot = lax.rem(t, 2)
    prepare = _dn_prepare({False: (qf_ref, kf_ref, vf_ref, gf_ref),
                           True: (qb_ref, kb_ref, vb_ref, gbb_ref)}, stage, slot)
    scan = _dn_scan({False: of_ref, True: ob_ref}, {False: sf_ref, True: sb_ref}, stage, 1 - slot)
    live = [prepare, scan]
    while live:
        for gen in list(live):
            if next(gen, StopIteration) is StopIteration:
                live.remove(gen)


def _delta_rule(qn, kn, v, gb):
    nb, seq, _ = qn.shape
    tg = DN_GROUP * CHUNK
    n = seq // tg
    total = nb * n
    n_prob = 2 * DN_GROUP * DN_HEADS

    def in_map(reverse):
        def index(t):
            tt = jnp.minimum(t, total - 1)
            i = lax.rem(tt, n)
            return (tt // n, n - 1 - i if reverse else i, 0)
        return index

    def out_map(reverse):
        def index(t):
            tt = jnp.maximum(t - 1, 0)
            i = lax.rem(tt, n)
            return (tt // n, n - 1 - i if reverse else i, 0)
        return index

    spec = lambda width, index: pl.BlockSpec((None, tg, width), index)
    widths = (DN_QK_W, DN_QK_W, DN_W, LANES)
    state = pltpu.VMEM((DN_HEADS, DN_DK, DN_DV), F32)
    return pl.pallas_call(
        functools.partial(_dn_kernel, n=n),
        grid=(total + 1,),
        in_specs=[spec(w, in_map(False)) for w in widths] + [spec(w, in_map(True)) for w in widths],
        out_specs=(spec(DN_W, out_map(False)), spec(DN_W, out_map(True))),
        out_shape=(jax.ShapeDtypeStruct((nb, seq, DN_W), F32),
                   jax.ShapeDtypeStruct((nb, seq, DN_W), F32)),
        scratch_shapes=[
            state, state,
            pltpu.VMEM((2, n_prob, DN_DK, DN_DK), BF16),
            pltpu.VMEM((2, n_prob, CHUNK, DN_DV), F32),
            pltpu.VMEM((2, n_prob, CHUNK, CHUNK), BF16),
            pltpu.VMEM((2, n_prob, DN_DK, CHUNK), BF16),
            pltpu.VMEM((2, n_prob, SUBLANES, LANES), F32),
        ],
        compiler_params=_compiler_params(("arbitrary",)),
        name="delta_rule",
    )(qn, kn, v, gb, qn, kn, v, gb)


def _post_kernel(x_ref, oa_ref, of_ref, ob_ref, z_ref, gate_ref, mod_ref, dng_ref, g2_ref,
                 fg_ref, wba_ref, wbd_ref, wo_ref, w1_ref, w2_ref, y_ref, *, final):
    od = of_ref[...] + ob_ref[...]
    z = z_ref[...].astype(F32)
    dng = dng_ref[...]
    parts = []
    for h in range(DN_HEADS):
        lanes = slice(h * DN_DV, (h + 1) * DN_DV)
        parts.append(_rms(od[:, lanes], dng) * _silu(z[:, lanes]))
    od_n = jnp.concatenate(parts, axis=1).astype(BF16)
    br_a = jnp.dot(oa_ref[...], wba_ref[...], preferred_element_type=F32)
    br_d = jnp.dot(od_n, wbd_ref[...], preferred_element_type=F32)
    gate_a = _sigmoid(gate_ref[:, 0:D_MODEL].astype(F32))
    gate_d = _sigmoid(gate_ref[:, D_MODEL:].astype(F32))
    merged = gate_a * br_a + gate_d * br_d
    mix = jnp.dot(merged.astype(BF16), wo_ref[...], preferred_element_type=F32)
    gt1 = mod_ref[:, 2 * D_MODEL:3 * D_MODEL]
    sh2 = mod_ref[:, 3 * D_MODEL:4 * D_MODEL]
    sc2 = mod_ref[:, 4 * D_MODEL:5 * D_MODEL]
    gt2 = mod_ref[:, 5 * D_MODEL:6 * D_MODEL]
    x = x_ref[...] + gt1 * mix
    h2 = (_rms(x, g2_ref[...]) * (1.0 + sc2) + sh2).astype(BF16)
    acc = jnp.zeros_like(x)
    for j in range(D_FF // D_MODEL):
        cols = slice(j * D_MODEL, (j + 1) * D_MODEL)
        hid = jnp.maximum(jnp.dot(h2, w1_ref[:, cols], preferred_element_type=F32), 0.0)
        acc = acc + jnp.dot((hid * hid).astype(BF16), w2_ref[cols, :], preferred_element_type=F32)
    x = x + gt2 * acc
    if final:
        x = _rms(x, fg_ref[...])
    y_ref[...] = x


def _post(x, oa, of, ob, z, gates, mod_l, dn_norm_g, norm_mlp_g, final_g, weights, final):
    nb, seq, _ = x.shape
    tm = TM_POST
    tok = lambda width: pl.BlockSpec((None, tm, width), lambda b, i: (b, i, 0))
    vec = lambda width: _resident((1, width), lambda b, i: (0, 0))
    full = lambda w: _resident(w.shape, lambda b, i: (0, 0))
    return pl.pallas_call(
        functools.partial(_post_kernel, final=final),
        grid=(nb, seq // tm),
        in_specs=[
            tok(D_MODEL), tok(NA_W), tok(DN_W), tok(DN_W), tok(DN_W), tok(N_GATE),
            pl.BlockSpec((None, 1, N_MOD * D_MODEL), lambda b, i: (b, 0, 0)),
            vec(DN_DV), vec(D_MODEL), vec(D_MODEL),
        ] + [full(w) for w in weights],
        out_specs=tok(D_MODEL),
        out_shape=jax.ShapeDtypeStruct((nb, seq, D_MODEL), F32),
        compiler_params=_compiler_params(("parallel", "parallel")),
        name="merge_mlp",
    )(x, oa, of, ob, z, gates, mod_l, dn_norm_g, norm_mlp_g, final_g, *weights)


def _split_w_in(w_in_l):
    o1 = 3 * NA_W
    o2 = o1 + DN_QKV_W
    o3 = o2 + DN_W
    o4 = o3 + AB_W
    wab = jnp.pad(w_in_l[:, o3:o4], ((0, 0), (0, LANES - AB_W)))
    wa = jnp.concatenate([w_in_l[:, :NA_W] * (NA_DH ** -0.5), w_in_l[:, NA_W:o1]], axis=1)
    parts = (wa, w_in_l[:, o1:o2], w_in_l[:, o2:o3], wab, w_in_l[:, o4:])
    return tuple(p.astype(BF16) for p in parts)


def _trunk(x, mod, p):
    for l in range(DEPTH):
        qa, ka, va, qn, kn, v, gb, z, gates = _in_projection(
            x, mod[l], p["norm_mix_g"][l], p["w_in"][l], p["dn_conv"][l], p["dn_a_log"][l],
            p["dn_dt_bias"][l])
        oa = _attention(qa, ka, va, p["bias"][l])
        of, ob = _delta_rule(qn, kn, v, gb)
        x = _post(x, oa, of, ob, z, gates, mod[l], p["dn_norm_g"][l], p["norm_mlp_g"][l],
                  p["final_norm_g"], p["post_w"][l], final=(l == DEPTH - 1))
    return x


def kernel(x_prompt, x_sample, c_prompt, c_sample, norm_mix_g, norm_mlp_g, w_ada, b_ada, w_in,
           na_rpb, dn_conv, dn_a_log, dn_dt_bias, dn_norm_g, w_br_attn, w_br_dn, w_out, w_mlp1,
           w_mlp2, final_norm_g):
    row = lambda t: t.reshape(1, -1).astype(F32)
    p = {
        "norm_mix_g": [row(norm_mix_g[l]) for l in range(DEPTH)],
        "norm_mlp_g": [row(norm_mlp_g[l]) for l in range(DEPTH)],
        "dn_norm_g": [row(dn_norm_g[l]) for l in range(DEPTH)],
        "final_norm_g": row(final_norm_g),
        "w_in": [_split_w_in(w_in[l]) for l in range(DEPTH)],
        "bias": [_attention_bias(na_rpb[l]) for l in range(DEPTH)],
        "dn_conv": [dn_conv[l].astype(F32) for l in range(DEPTH)],
        "dn_a_log": dn_a_log,
        "dn_dt_bias": dn_dt_bias,
        "post_w": [tuple(w[l].astype(BF16) for w in (w_br_attn, w_br_dn, w_out, w_mlp1, w_mlp2))
                   for l in range(DEPTH)],
    }
    n_prompt = c_prompt.shape[0]
    mod = _modulation(jnp.concatenate([c_prompt, c_sample], axis=0), w_ada, b_ada)
    mod = mod.reshape(DEPTH, -1, 1, N_MOD * D_MODEL)
    y_prompt = _trunk(x_prompt, mod[:, :n_prompt], p)
    y_sample = _trunk(x_sample, mod[:, n_prompt:], p)
    return (y_prompt, y_sample)
```

```python
import functools

import jax
import jax.numpy as jnp
import numpy as np
from jax import lax
from jax.experimental import pallas as pl
from jax.experimental.pallas import tpu as pltpu

D_MODEL = 1024
DEPTH = 2
GRID_W = 64
NA_HEADS = 8
NA_DH = 64
NA_W = NA_HEADS * NA_DH
NA_KH = 8
NA_KW = 16
DN_HEADS = 4
DN_DK = 128
DN_DV = 128
DN_QK_W = DN_HEADS * DN_DK
DN_W = DN_HEADS * DN_DV
DN_QKV_W = 2 * DN_QK_W + DN_W
CONV_W = 5
CHUNK = 64
D_FF = 4 * D_MODEL
N_MOD = 6
EPS = 1e-6
N_GATE = 2 * D_MODEL
AB_W = 4 * DN_HEADS
GB_BETA = 0
GB_G = 2 * DN_HEADS
GB_CUM = 4 * DN_HEADS
GB_TOT = 6 * DN_HEADS
GB_END = 8 * DN_HEADS

LANES = 128
SUBLANES = 8
MXU_N = 256
VMEM_LIMIT_BYTES = 56 * 1024 * 1024

TM_IN = 512
TM_POST = 512
ATT_ROWS = 8
ATT_WIN_ROWS = 3 * ATT_ROWS
ATT_UNROLL = 8
DN_GROUP = 8
HALO = 2 * SUBLANES

BF16 = jnp.bfloat16
F32 = jnp.float32
HIGHEST = lax.Precision.HIGHEST


def _compiler_params(semantics):
    return pltpu.CompilerParams(dimension_semantics=semantics,
                                vmem_limit_bytes=VMEM_LIMIT_BYTES)


def _resident(block_shape, index_map):
    return pl.BlockSpec(block_shape, index_map, pipeline_mode=pl.Buffered(1))


def _silu(x):
    half = 0.5 * x
    return half * (jnp.tanh(half) + 1.0)


def _sigmoid(x):
    return 0.5 * (jnp.tanh(0.5 * x) + 1.0)


def _rms(x, g):
    return (x * lax.rsqrt(jnp.mean(x * x, axis=-1, keepdims=True) + EPS)) * g


def _mod_kernel(c_ref, w_ref, b_ref, o_ref):
    c = _silu(c_ref[...])
    o_ref[...] = jnp.dot(c, w_ref[...], precision=HIGHEST,
                         preferred_element_type=F32) + b_ref[...]


def _modulation(c, w_ada, b_ada):
    nb = c.shape[0]
    tn = 1024
    n_out = N_MOD * D_MODEL
    return pl.pallas_call(
        _mod_kernel,
        grid=(DEPTH, n_out // tn),
        in_specs=[
            pl.BlockSpec((nb, D_MODEL), lambda l, j: (0, 0)),
            pl.BlockSpec((None, D_MODEL, tn), lambda l, j: (l, 0, j)),
            pl.BlockSpec((None, 1, tn), lambda l, j: (l, 0, j)),
        ],
        out_specs=pl.BlockSpec((None, nb, tn), lambda l, j: (l, 0, j)),
        out_shape=jax.ShapeDtypeStruct((DEPTH, nb, n_out), F32),
        compiler_params=_compiler_params(("arbitrary", "arbitrary")),
        name="adaln_mod",
    )(c, w_ada, b_ada.reshape(DEPTH, 1, n_out))


def _chunk_scan(x, reverse):
    tm = x.shape[0]
    pos = lax.broadcasted_iota(jnp.int32, x.shape, 0) & (CHUNK - 1)
    step = 1
    while step < CHUNK:
        if reverse:
            shifted = pltpu.roll(x, tm - step, axis=0)
            x = x + jnp.where(pos < CHUNK - step, shifted, 0.0)
        else:
            shifted = pltpu.roll(x, step, axis=0)
            x = x + jnp.where(pos >= step, shifted, 0.0)
        step *= 2
    return x


def _in_kernel(x_ref, xprev_ref, xnext_ref, mod_ref, g_ref, wd_ref, wa_ref, wz_ref, wab_ref,
               wg_ref, cw_ref, alog_ref, dtb_ref,
               qa_ref, ka_ref, va_ref, qn_ref, kn_ref, v_ref, gb_ref, z_ref, gate_ref,
               pad_ref, h_ref, *, tm):
    i = pl.program_id(1)
    n = pl.num_programs(1)
    sh1 = mod_ref[:, 0:D_MODEL]
    sc1 = mod_ref[:, D_MODEL:2 * D_MODEL]
    gain = g_ref[...]
    normed = lambda xv: (_rms(xv, gain) * (1.0 + sc1) + sh1).astype(BF16)
    h_ref[0:HALO, :] = normed(xprev_ref[...])
    h_ref[HALO:HALO + tm, :] = normed(x_ref[...])
    h_ref[HALO + tm:, :] = normed(xnext_ref[...])
    main = slice(HALO, HALO + tm)

    def deltanet_block(c):
        cols = slice(c * MXU_N, (c + 1) * MXU_N)
        d = jnp.dot(h_ref[...], wd_ref[:, cols], preferred_element_type=F32)
        pad_ref[0:HALO, cols] = jnp.where(i > 0, d[0:HALO], 0.0)
        pad_ref[main, cols] = d[main]
        pad_ref[HALO + tm:, cols] = jnp.where(i < n - 1, d[HALO + tm:], 0.0)
        first = HALO - CONV_W // 2
        acc = pad_ref[first:first + tm, cols] * cw_ref[0:1, cols]
        for t in range(1, CONV_W):
            acc = acc + pad_ref[first + t:first + t + tm, cols] * cw_ref[t:t + 1, cols]
        y = _silu(acc)
        out_ref, scale = ((qn_ref, DN_DK ** -0.5), (kn_ref, None), (v_ref, None))[c * MXU_N // DN_QK_W]
        base = c * MXU_N % DN_QK_W
        for hd in range(MXU_N // DN_DK):
            yh = y[:, hd * DN_DK:(hd + 1) * DN_DK]
            if out_ref is not v_ref:
                yh = yh * lax.rsqrt(jnp.sum(yh * yh, axis=-1, keepdims=True) + EPS)
            if scale is not None:
                yh = yh * scale
            out_ref[:, base + hd * DN_DK:base + (hd + 1) * DN_DK] = yh

    def plain_block(w_ref, out_ref, c):
        cols = slice(c * MXU_N, (c + 1) * MXU_N)
        r = jnp.dot(h_ref[main, :], w_ref[:, cols], preferred_element_type=F32)
        out_ref[:, cols] = r.astype(BF16)

    others = [functools.partial(plain_block, w_ref, out_ref, c)
              for w_ref, out_ref, width in ((wa_ref, qa_ref, NA_W),
                                            (wa_ref.at[:, NA_W:2 * NA_W], ka_ref, NA_W),
                                            (wa_ref.at[:, 2 * NA_W:], va_ref, NA_W),
                                            (wz_ref, z_ref, DN_W), (wg_ref, gate_ref, N_GATE))
              for c in range(width // MXU_N)]
    ab = jnp.dot(h_ref[main, :], wab_ref[...], preferred_element_type=F32)
    n_dn = DN_QKV_W // MXU_N
    per_dn = 1
    for c in range(n_dn):
        deltanet_block(c)
        for emit in others[c * per_dn:(c + 1) * per_dn]:
            emit()
    for emit in others[n_dn * per_dn:]:
        emit()

    beta = _sigmoid(ab)
    a = ab + dtb_ref[...]
    softplus = jnp.maximum(a, 0.0) + jnp.log1p(jnp.exp(-jnp.abs(a)))
    g = -jnp.exp(alog_ref[...]) * softplus
    col = lax.broadcasted_iota(jnp.int32, ab.shape, 1)
    raw = jnp.where(col < GB_G, beta, jnp.where(col < GB_CUM, g, 0.0))
    prefix = _chunk_scan(raw, reverse=False)
    suffix = _chunk_scan(raw, reverse=True)
    total = pltpu.roll(prefix + suffix - raw, GB_TOT - GB_G, axis=1)
    cum = jnp.where(col < GB_CUM + DN_HEADS, pltpu.roll(prefix, GB_CUM - GB_G, axis=1),
                    pltpu.roll(suffix, GB_CUM - GB_G, axis=1))
    gb_ref[...] = jnp.where(col < GB_CUM, raw,
                            jnp.where(col < GB_TOT, cum, jnp.where(col < GB_END, total, 0.0)))


def _in_projection(x, mod_l, norm_g, w_parts, conv_w, a_log, dt_bias):
    nb, seq, _ = x.shape
    tm = TM_IN
    per = tm // HALO
    n_halo = seq // HALO
    wa, wd, wz, wab, wg = w_parts
    pad_cols = lambda t: jnp.pad(t.reshape(1, 2 * DN_HEADS).astype(F32),
                                 ((0, 0), (GB_G, LANES - GB_CUM)))
    tok = lambda width: pl.BlockSpec((None, tm, width), lambda b, i: (b, i, 0))
    full = lambda w: _resident(w.shape, lambda b, i: (0, 0))
    vec = lambda width: _resident((1, width), lambda b, i: (0, 0))
    out_shape = (
        jax.ShapeDtypeStruct((nb, seq, NA_W), BF16),
        jax.ShapeDtypeStruct((nb, seq, NA_W), BF16),
        jax.ShapeDtypeStruct((nb, seq, NA_W), BF16),
        jax.ShapeDtypeStruct((nb, seq, DN_QK_W), F32),
        jax.ShapeDtypeStruct((nb, seq, DN_QK_W), F32),
        jax.ShapeDtypeStruct((nb, seq, DN_W), F32),
        jax.ShapeDtypeStruct((nb, seq, LANES), F32),
        jax.ShapeDtypeStruct((nb, seq, DN_W), BF16),
        jax.ShapeDtypeStruct((nb, seq, N_GATE), BF16),
    )
    return pl.pallas_call(
        functools.partial(_in_kernel, tm=tm),
        grid=(nb, seq // tm),
        in_specs=[
            tok(D_MODEL),
            pl.BlockSpec((None, HALO, D_MODEL), lambda b, i: (b, jnp.maximum(i * per - 1, 0), 0)),
            pl.BlockSpec((None, HALO, D_MODEL),
                         lambda b, i: (b, jnp.minimum((i + 1) * per, n_halo - 1), 0)),
            pl.BlockSpec((None, 1, N_MOD * D_MODEL), lambda b, i: (b, 0, 0)),
            vec(D_MODEL),
            full(wd), full(wa), full(wz), full(wab), full(wg),
            _resident((CONV_W, DN_QKV_W), lambda b, i: (0, 0)),
            vec(LANES), vec(LANES),
        ],
        out_specs=(tok(NA_W), tok(NA_W), tok(NA_W), tok(DN_QK_W), tok(DN_QK_W), tok(DN_W),
                   tok(LANES), tok(DN_W), tok(N_GATE)),
        out_shape=out_shape,
        scratch_shapes=[pltpu.VMEM((tm + 2 * HALO, DN_QKV_W), F32),
                        pltpu.VMEM((tm + 2 * HALO, D_MODEL), BF16)],
        compiler_params=_compiler_params(("parallel", "parallel")),
        name="in_projection",
    )(x, x, x, mod_l, norm_g, wd, wa, wz, wab, wg, conv_w, pad_cols(a_log), pad_cols(dt_bias))


def _att_kernel(q_ref, k_ref, v_ref, bias_ref, o_ref, s_ref, *, n_blocks):
    j = pl.program_id(1)
    rows = n_blocks * ATT_ROWS
    base = jnp.clip(j - 1, 0, n_blocks - 3)
    lane = lax.broadcasted_iota(jnp.int32, (GRID_W, LANES), 1)
    low = lane < NA_DH

    def rows_body(it, carry):
        probs = []
        for u in range(ATT_UNROLL):
            ri = it * ATT_UNROLL + u
            r = j * ATT_ROWS + ri
            sr = jnp.clip(r - NA_KH // 2, 0, rows - NA_KH)
            delta = r - sr
            off = pl.multiple_of((sr - base * ATT_ROWS) * GRID_W, GRID_W)
            qrow = pl.multiple_of(ri * GRID_W, GRID_W)
            for p in range(NA_HEADS // 2):
                lanes = slice(p * LANES, (p + 1) * LANES)
                qp = q_ref[pl.ds(qrow, GRID_W), lanes]
                zero = jnp.zeros_like(qp)
                lhs = jnp.concatenate([jnp.where(low, qp, zero), jnp.where(low, zero, qp)], axis=0)
                kp = k_ref[0, pl.ds(off, NA_KH * GRID_W), lanes]
                s_ref[len(probs)] = lax.dot_general(lhs, kp, (((1,), (1,)), ((), ())),
                                                    preferred_element_type=F32)
                probs.append((qrow, off, delta, p, lanes, len(probs)))
        pvs = []
        for qrow, off, delta, p, lanes, slot in probs:
            s = s_ref[slot] + bias_ref[delta, p]
            m = jnp.max(s, axis=-1, keepdims=True)
            e = jnp.exp(s - m)
            den = jnp.sum(e, axis=-1, keepdims=True)
            vp = v_ref[0, pl.ds(off, NA_KH * GRID_W), lanes]
            pvs.append((jnp.dot(e.astype(BF16), vp, preferred_element_type=F32), den))
        for (qrow, off, delta, p, lanes, slot), (pv, den) in zip(probs, pvs):
            pv = pv / den
            o_ref[pl.ds(qrow, GRID_W), lanes] = jnp.where(low, pv[0:GRID_W], pv[GRID_W:]).astype(BF16)
        return carry

    lax.fori_loop(0, ATT_ROWS // ATT_UNROLL, rows_body, 0)


def _attention(qa, ka, va, bias):
    nb, seq, _ = qa.shape
    blk = ATT_ROWS * GRID_W
    n_blocks = seq // blk
    assert n_blocks >= 3
    win = ATT_WIN_ROWS * GRID_W

    def win_map(b, j):
        return (b, jnp.clip(j - 1, 0, n_blocks - 3) * blk, 0)

    win_spec = pl.BlockSpec((pl.Element(1), pl.Element(win), pl.Element(NA_W)), win_map)
    return pl.pallas_call(
        functools.partial(_att_kernel, n_blocks=n_blocks),
        grid=(nb, n_blocks),
        in_specs=[
            pl.BlockSpec((None, blk, NA_W), lambda b, j: (b, j, 0)),
            win_spec, win_spec,
            _resident(bias.shape, lambda b, j: (0, 0, 0, 0)),
        ],
        out_specs=pl.BlockSpec((None, blk, NA_W), lambda b, j: (b, j, 0)),
        out_shape=jax.ShapeDtypeStruct((nb, seq, NA_W), BF16),
        scratch_shapes=[pltpu.VMEM((ATT_UNROLL * NA_HEADS // 2, 2 * GRID_W, NA_KH * GRID_W), F32)],
        compiler_params=_compiler_params(("parallel", "parallel")),
        name="neighborhood_attention",
    )(qa, ka, va, bias)


def _attention_bias(rpb):
    qc = np.arange(GRID_W)
    kc = np.arange(GRID_W)
    start = np.clip(qc - NA_KW // 2, 0, GRID_W - NA_KW)
    valid = (kc[None, :] >= start[:, None]) & (kc[None, :] < start[:, None] + NA_KW)
    neg = jnp.asarray(np.where(valid, 0.0, -1e30).astype(np.float32))
    dc = np.clip(kc[None, :] - qc[:, None] + NA_KW - 1, 0, 2 * NA_KW - 2)
    delta = np.arange(NA_KH)
    dr = np.arange(NA_KH)[None, :] - delta[:, None] + (NA_KH - 1)
    onehot = jnp.asarray((dc[None] == np.arange(2 * NA_KW - 1)[:, None, None]).astype(np.float32))
    t = jnp.einsum('hdic,cqk->dhqik', rpb.astype(F32)[:, dr], onehot, precision=HIGHEST)
    t = t + neg[None, None, :, None, :]
    return t.reshape(NA_KH, NA_HEADS // 2, 2 * GRID_W, NA_KH * GRID_W)


_LOG2_CHUNK = CHUNK.bit_length() - 1


def _dn_prepare(refs, stage, slot):
    row = lax.broadcasted_iota(jnp.int32, (CHUNK, CHUNK), 0)
    colm = lax.broadcasted_iota(jnp.int32, (CHUNK, CHUNK), 1)
    nt = (((1,), (1,)), ((), ()))
    probs = []
    for reverse in (False, True):
        q_ref, k_ref, v_ref, g_ref = refs[reverse]
        incl, strict = (colm >= row, colm > row) if reverse else (colm <= row, colm < row)
        for c in range(DN_GROUP):
            rows = slice(c * CHUNK, (c + 1) * CHUNK)
            gb = g_ref[rows, :]
            gb_t = gb.T
            for h in range(DN_HEADS):
                lanes = slice(h * DN_DK, (h + 1) * DN_DK)
                hd = h + (DN_HEADS if reverse else 0)
                q, k, v = q_ref[rows, lanes], k_ref[rows, lanes], v_ref[rows, lanes]
                q_b, k_b = q.astype(BF16), k.astype(BF16)
                probs.append(dict(
                    reverse=reverse, incl=incl, strict=strict, q=q, k=k, v=v,
                    beta=gb[:, GB_BETA + hd:GB_BETA + hd + 1],
                    g_col=gb[:, GB_CUM + hd:GB_CUM + hd + 1],
                    g_row=gb_t[GB_CUM + hd:GB_CUM + hd + 1, :],
                    g_tot_row=gb_t[GB_TOT + hd:GB_TOT + hd + 1, :],
                    qk_kk=lax.dot_general(jnp.concatenate([q_b, k_b], axis=0), k_b, nt,
                                          preferred_element_type=F32)))
    yield
    for i, p in enumerate(probs):
        beta = jnp.broadcast_to(p["beta"], (CHUNK, DN_DK))
        g_col = jnp.broadcast_to(p["g_col"], (CHUNK, DN_DK))
        g_tot = jnp.broadcast_to(p["g_tot_row"], (CHUNK, CHUNK))
        g_tot = jnp.concatenate([g_tot, g_tot], axis=1)
        decay = jnp.exp(jnp.where(p["incl"], g_col[:, :CHUNK] - p["g_row"], -jnp.inf))
        stage["qk"][slot, i] = (p["qk_kk"][:CHUNK] * decay).astype(BF16)
        p["a"] = jnp.where(p["strict"], beta[:, :CHUNK] * p["qk_kk"][CHUNK:] * decay, 0.0)
        exp_g = jnp.exp(g_col)
        p["x"] = jnp.concatenate([(beta * exp_g) * p["k"], beta * p["v"]], axis=1)
        p["q_dec"] = p["q"] * exp_g
        k_dec = p["k"] * jnp.exp(g_tot - g_col)
        stage["kdt"][slot, i] = k_dec.T.astype(BF16)
        stage["gl"][slot, i] = jnp.exp(g_tot[0:SUBLANES])
    wide = DN_HEADS * CHUNK
    diag = ((lax.broadcasted_iota(jnp.int32, (wide, wide), 0) >> _LOG2_CHUNK)
            == (lax.broadcasted_iota(jnp.int32, (wide, wide), 1) >> _LOG2_CHUNK))

    def block_diag(m):
        m = jnp.concatenate([m] * DN_HEADS, axis=0)
        return jnp.where(diag, m, 0.0).astype(BF16)

    ri = lax.broadcasted_iota(jnp.int32, (CHUNK, wide), 0)
    ci = lax.broadcasted_iota(jnp.int32, (CHUNK, wide), 1) & (CHUNK - 1)
    groups = []
    for g0 in range(0, len(probs), DN_HEADS):
        members = probs[g0:g0 + DN_HEADS]
        groups.append(dict(members=members, first=g0, reverse=members[0]["reverse"],
                           a=jnp.concatenate([p["a"] for p in members], axis=1)))
    for level in range(_LOG2_CHUNK):
        for g in groups:
            lo_i, lo_j = ((ri >> level) & 1) == 1, ((ci >> level) & 1) == 0
            couple = ((ri >> (level + 1)) == (ci >> (level + 1))) & (
                (~lo_i & ~lo_j) if g["reverse"] else (lo_i & lo_j))
            g["l"] = jnp.where(couple, g["a"], 0.0)
        if level == 0:
            for g in groups:
                g["e"] = -g["l"]
            continue
        prods = [jnp.dot(g["l"].astype(BF16), block_diag(g["e"]), preferred_element_type=F32)
                 for g in groups]
        yield
        for g, r in zip(groups, prods):
            g["m"] = g["l"] + r
        prods = [jnp.dot(g["e"].astype(BF16), block_diag(g["m"]), preferred_element_type=F32)
                 for g in groups]
        yield
        for g, r in zip(groups, prods):
            g["e"] = g["e"] - (g["m"] + r)
    low = lax.broadcasted_iota(jnp.int32, (CHUNK, 2 * CHUNK), 1) < CHUNK
    prods = []
    for g in groups:
        for pair in range(DN_HEADS // 2):
            e = g["e"][:, pair * 2 * CHUNK:(pair + 1) * 2 * CHUNK]
            lhs = jnp.concatenate([jnp.where(low, e, 0.0), jnp.where(low, 0.0, e)], axis=0)
            rhs = jnp.concatenate([p["x"] for p in g["members"][2 * pair:2 * pair + 2]], axis=0)
            prods.append(jnp.dot(lhs.astype(BF16), rhs.astype(BF16), preferred_element_type=F32))
    yield
    for i, p in enumerate(probs):
        r = prods[i // 2]
        x = p["x"] + (r[CHUNK:] if i % 2 else r[:CHUNK])
        stage["wq"][slot, i] = jnp.concatenate([x[:, :DN_DK], p["q_dec"]], axis=0).astype(BF16)
        stage["u"][slot, i] = x[:, DN_DK:]


def _dn_scan(out_refs, state_refs, stage, slot):
    states = {(rev, h): state_refs[rev][h] for rev in (False, True) for h in range(DN_HEADS)}
    for step in range(DN_GROUP):
        chains = []
        for reverse in (False, True):
            c = DN_GROUP - 1 - step if reverse else step
            for h in range(DN_HEADS):
                chains.append((reverse, c, h, ((DN_GROUP if reverse else 0) + c) * DN_HEADS + h))
        ws_qs = [jnp.dot(stage["wq"][slot, i], states[rev, h].astype(BF16),
                         preferred_element_type=F32) for rev, c, h, i in chains]
        yield
        v_new = [(stage["u"][slot, i] - r[:CHUNK]).astype(BF16)
                 for (rev, c, h, i), r in zip(chains, ws_qs)]
        intra = [jnp.dot(stage["qk"][slot, i], vn, preferred_element_type=F32)
                 for (rev, c, h, i), vn in zip(chains, v_new)]
        ds = [jnp.dot(stage["kdt"][slot, i], vn, preferred_element_type=F32)
              for (rev, c, h, i), vn in zip(chains, v_new)]
        yield
        for (rev, c, h, i), r, oi, d in zip(chains, ws_qs, intra, ds):
            out_refs[rev][c * CHUNK:(c + 1) * CHUNK, h * DN_DV:(h + 1) * DN_DV] = r[CHUNK:] + oi
            g_last = jnp.concatenate([stage["gl"][slot, i]] * (DN_DK // SUBLANES), axis=0)
            states[rev, h] = states[rev, h] * g_last + d
    for (rev, h), s in states.items():
        state_refs[rev][h] = s


def _dn_kernel(qf_ref, kf_ref, vf_ref, gf_ref, qb_ref, kb_ref, vb_ref, gbb_ref,
               of_ref, ob_ref, sf_ref, sb_ref, wq_ref, u_ref, qk_ref, kdt_ref, gl_ref, *, n):
    t = pl.program_id(0)
    stage = dict(wq=wq_ref, u=u_ref, qk=qk_ref, kdt=kdt_ref, gl=gl_ref)

    @pl.when(t == 0)
    def _():
        for ref in stage.values():
            ref[...] = jnp.zeros_like(ref)

    @pl.when((t == 0) | (lax.rem(t + n - 1, n) == 0))
    def _():
        sf_ref[...] = jnp.zeros_like(sf_ref)
        sb_ref[...] = jnp.zeros_like(sb_ref)

    slot = lax.rem(t, 2)
    prepare = _dn_prepare({False: (qf_ref, kf_ref, vf_ref, gf_ref),
                           True: (qb_ref, kb_ref, vb_ref, gbb_ref)}, stage, slot)
    scan = _dn_scan({False: of_ref, True: ob_ref}, {False: sf_ref, True: sb_ref}, stage, 1 - slot)
    live = [prepare, scan]
    while live:
        for gen in list(live):
            if next(gen, StopIteration) is StopIteration:
                live.remove(gen)


def _delta_rule(qn, kn, v, gb):
    nb, seq, _ = qn.shape
    tg = DN_GROUP * CHUNK
    n = seq // tg
    total = nb * n
    n_prob = 2 * DN_GROUP * DN_HEADS

    def in_map(reverse):
        def index(t):
            tt = jnp.minimum(t, total - 1)
            i = lax.rem(tt, n)
            return (tt // n, n - 1 - i if reverse else i, 0)
        return index

    def out_map(reverse):
        def index(t):
            tt = jnp.maximum(t - 1, 0)
            i = lax.rem(tt, n)
            return (tt // n, n - 1 - i if reverse else i, 0)
        return index

    spec = lambda width, index: pl.BlockSpec((None, tg, width), index)
    widths = (DN_QK_W, DN_QK_W, DN_W, LANES)
    state = pltpu.VMEM((DN_HEADS, DN_DK, DN_DV), F32)
    return pl.pallas_call(
        functools.partial(_dn_kernel, n=n),
        grid=(total + 1,),
        in_specs=[spec(w, in_map(False)) for w in widths] + [spec(w, in_map(True)) for w in widths],
        out_specs=(spec(DN_W, out_map(False)), spec(DN_W, out_map(True))),
        out_shape=(jax.ShapeDtypeStruct((nb, seq, DN_W), F32),
                   jax.ShapeDtypeStruct((nb, seq, DN_W), F32)),
        scratch_shapes=[
            state, state,
            pltpu.VMEM((2, n_prob, DN_DK, DN_DK), BF16),
            pltpu.VMEM((2, n_prob, CHUNK, DN_DV), F32),
            pltpu.VMEM((2, n_prob, CHUNK, CHUNK), BF16),
            pltpu.VMEM((2, n_prob, DN_DK, CHUNK), BF16),
            pltpu.VMEM((2, n_prob, SUBLANES, LANES), F32),
        ],
        compiler_params=_compiler_params(("arbitrary",)),
        name="delta_rule",
    )(qn, kn, v, gb, qn, kn, v, gb)


def _post_kernel(x_ref, oa_ref, of_ref, ob_ref, z_ref, gate_ref, mod_ref, dng_ref, g2_ref,
                 fg_ref, wba_ref, wbd_ref, wo_ref, w1_ref, w2_ref, y_ref, *, final):
    od = of_ref[...] + ob_ref[...]
    z = z_ref[...].astype(F32)
    dng = dng_ref[...]
    parts = []
    for h in range(DN_HEADS):
        lanes = slice(h * DN_DV, (h + 1) * DN_DV)
        parts.append(_rms(od[:, lanes], dng) * _silu(z[:, lanes]))
    od_n = jnp.concatenate(parts, axis=1).astype(BF16)
    br_a = jnp.dot(oa_ref[...], wba_ref[...], preferred_element_type=F32)
    br_d = jnp.dot(od_n, wbd_ref[...], preferred_element_type=F32)
    gate_a = _sigmoid(gate_ref[:, 0:D_MODEL].astype(F32))
    gate_d = _sigmoid(gate_ref[:, D_MODEL:].astype(F32))
    merged = gate_a * br_a + gate_d * br_d
    mix = jnp.dot(merged.astype(BF16), wo_ref[...], preferred_element_type=F32)
    gt1 = mod_ref[:, 2 * D_MODEL:3 * D_MODEL]
    sh2 = mod_ref[:, 3 * D_MODEL:4 * D_MODEL]
    sc2 = mod_ref[:, 4 * D_MODEL:5 * D_MODEL]
    gt2 = mod_ref[:, 5 * D_MODEL:6 * D_MODEL]
    x = x_ref[...] + gt1 * mix
    h2 = (_rms(x, g2_ref[...]) * (1.0 + sc2) + sh2).astype(BF16)
    acc = jnp.zeros_like(x)
    for j in range(D_FF // D_MODEL):
        cols = slice(j * D_MODEL, (j + 1) * D_MODEL)
        hid = jnp.maximum(jnp.dot(h2, w1_ref[:, cols], preferred_element_type=F32), 0.0)
        acc = acc + jnp.dot((hid * hid).astype(BF16), w2_ref[cols, :], preferred_element_type=F32)
    x = x + gt2 * acc
    if final:
        x = _rms(x, fg_ref[...])
    y_ref[...] = x


def _post(x, oa, of, ob, z, gates, mod_l, dn_norm_g, norm_mlp_g, final_g, weights, final):
    nb, seq, _ = x.shape
    tm = TM_POST
    tok = lambda width: pl.BlockSpec((None, tm, width), lambda b, i: (b, i, 0))
    vec = lambda width: _resident((1, width), lambda b, i: (0, 0))
    full = lambda w: _resident(w.shape, lambda b, i: (0, 0))
    return pl.pallas_call(
        functools.partial(_post_kernel, final=final),
        grid=(nb, seq // tm),
        in_specs=[
            tok(D_MODEL), tok(NA_W), tok(DN_W), tok(DN_W), tok(DN_W), tok(N_GATE),
            pl.BlockSpec((None, 1, N_MOD * D_MODEL), lambda b, i: (b, 0, 0)),
            vec(DN_DV), vec(D_MODEL), vec(D_MODEL),
        ] + [full(w) for w in weights],
        out_specs=tok(D_MODEL),
        out_shape=jax.ShapeDtypeStruct((nb, seq, D_MODEL), F32),
        compiler_params=_compiler_params(("parallel", "parallel")),
        name="merge_mlp",
    )(x, oa, of, ob, z, gates, mod_l, dn_norm_g, norm_mlp_g, final_g, *weights)


def _split_w_in(w_in_l):
    o1 = 3 * NA_W
    o2 = o1 + DN_QKV_W
    o3 = o2 + DN_W
    o4 = o3 + AB_W
    wab = jnp.pad(w_in_l[:, o3:o4], ((0, 0), (0, LANES - AB_W)))
    wa = jnp.concatenate([w_in_l[:, :NA_W] * (NA_DH ** -0.5), w_in_l[:, NA_W:o1]], axis=1)
    parts = (wa, w_in_l[:, o1:o2], w_in_l[:, o2:o3], wab, w_in_l[:, o4:])
    return tuple(p.astype(BF16) for p in parts)


def _trunk(x, mod, p):
    for l in range(DEPTH):
        qa, ka, va, qn, kn, v, gb, z, gates = _in_projection(
            x, mod[l], p["norm_mix_g"][l], p["w_in"][l], p["dn_conv"][l], p["dn_a_log"][l],
            p["dn_dt_bias"][l])
        oa = _attention(qa, ka, va, p["bias"][l])
        of, ob = _delta_rule(qn, kn, v, gb)
        x = _post(x, oa, of, ob, z, gates, mod[l], p["dn_norm_g"][l], p["norm_mlp_g"][l],
                  p["final_norm_g"], p["post_w"][l], final=(l == DEPTH - 1))
    return x


def kernel(x_prompt, x_sample, c_prompt, c_sample, norm_mix_g, norm_mlp_g, w_ada, b_ada, w_in,
           na_rpb, dn_conv, dn_a_log, dn_dt_bias, dn_norm_g, w_br_attn, w_br_dn, w_out, w_mlp1,
           w_mlp2, final_norm_g):
    row = lambda t: t.reshape(1, -1).astype(F32)
    p = {
        "norm_mix_g": [row(norm_mix_g[l]) for l in range(DEPTH)],
        "norm_mlp_g": [row(norm_mlp_g[l]) for l in range(DEPTH)],
        "dn_norm_g": [row(dn_norm_g[l]) for l in range(DEPTH)],
        "final_norm_g": row(final_norm_g),
        "w_in": [_split_w_in(w_in[l]) for l in range(DEPTH)],
        "bias": [_attention_bias(na_rpb[l]) for l in range(DEPTH)],
        "dn_conv": [dn_conv[l].astype(F32) for l in range(DEPTH)],
        "dn_a_log": dn_a_log,
        "dn_dt_bias": dn_dt_bias,
        "post_w": [tuple(w[l].astype(BF16) for w in (w_br_attn, w_br_dn, w_out, w_mlp1, w_mlp2))
                   for l in range(DEPTH)],
    }
    n_prompt = c_prompt.shape[0]
    mod = _modulation(jnp.concatenate([c_prompt, c_sample], axis=0), w_ada, b_ada)
    mod = mod.reshape(DEPTH, -1, 1, N_MOD * D_MODEL)
    y_prompt = _trunk(x_prompt, mod[:, :n_prompt], p)
    y_sample = _trunk(x_sample, mod[:, n_prompt:], p)
    return (y_prompt, y_sample)
```

```python
import functools

import jax
import jax.numpy as jnp
import numpy as np
from jax import lax
from jax.experimental import pallas as pl
from jax.experimental.pallas import tpu as pltpu

D_MODEL = 1024
DEPTH = 2
GRID_W = 64
NA_HEADS = 8
NA_DH = 64
NA_W = NA_HEADS * NA_DH
NA_KH = 8
NA_KW = 16
DN_HEADS = 4
DN_DK = 128
DN_DV = 128
DN_QK_W = DN_HEADS * DN_DK
DN_W = DN_HEADS * DN_DV
DN_QKV_W = 2 * DN_QK_W + DN_W
CONV_W = 5
CHUNK = 64
D_FF = 4 * D_MODEL
N_MOD = 6
EPS = 1e-6
N_GATE = 2 * D_MODEL
AB_W = 4 * DN_HEADS
GB_BETA = 0
GB_G = 2 * DN_HEADS
GB_CUM = 4 * DN_HEADS
GB_TOT = 6 * DN_HEADS
GB_END = 8 * DN_HEADS

LANES = 128
SUBLANES = 8
MXU_N = 256
VMEM_LIMIT_BYTES = 56 * 1024 * 1024

TM_IN = 512
TM_POST = 512
ATT_ROWS = 8
ATT_WIN_ROWS = 3 * ATT_ROWS
ATT_UNROLL = 8
DN_GROUP = 8
HALO = SUBLANES

BF16 = jnp.bfloat16
F32 = jnp.float32
HIGHEST = lax.Precision.HIGHEST


def _compiler_params(semantics):
    return pltpu.CompilerParams(dimension_semantics=semantics,
                                vmem_limit_bytes=VMEM_LIMIT_BYTES)


def _resident(block_shape, index_map):
    return pl.BlockSpec(block_shape, index_map, pipeline_mode=pl.Buffered(1))


def _silu(x):
    half = 0.5 * x
    return half * (jnp.tanh(half) + 1.0)


def _sigmoid(x):
    return 0.5 * (jnp.tanh(0.5 * x) + 1.0)


def _rms(x, g):
    return (x * lax.rsqrt(jnp.mean(x * x, axis=-1, keepdims=True) + EPS)) * g


def _mod_kernel(c_ref, w_ref, b_ref, o_ref):
    c = _silu(c_ref[...])
    o_ref[...] = jnp.dot(c, w_ref[...], precision=HIGHEST,
                         preferred_element_type=F32) + b_ref[...]


def _modulation(c, w_ada, b_ada):
    nb = c.shape[0]
    tn = 1024
    n_out = N_MOD * D_MODEL
    return pl.pallas_call(
        _mod_kernel,
        grid=(DEPTH, n_out // tn),
        in_specs=[
            pl.BlockSpec((nb, D_MODEL), lambda l, j: (0, 0)),
            pl.BlockSpec((None, D_MODEL, tn), lambda l, j: (l, 0, j)),
            pl.BlockSpec((None, 1, tn), lambda l, j: (l, 0, j)),
        ],
        out_specs=pl.BlockSpec((None, nb, tn), lambda l, j: (l, 0, j)),
        out_shape=jax.ShapeDtypeStruct((DEPTH, nb, n_out), F32),
        compiler_params=_compiler_params(("arbitrary", "arbitrary")),
        name="adaln_mod",
    )(c, w_ada, b_ada.reshape(DEPTH, 1, n_out))


def _chunk_scan(x, reverse):
    tm = x.shape[0]
    pos = lax.broadcasted_iota(jnp.int32, x.shape, 0) & (CHUNK - 1)
    step = 1
    while step < CHUNK:
        if reverse:
            shifted = pltpu.roll(x, tm - step, axis=0)
            x = x + jnp.where(pos < CHUNK - step, shifted, 0.0)
        else:
            shifted = pltpu.roll(x, step, axis=0)
            x = x + jnp.where(pos >= step, shifted, 0.0)
        step *= 2
    return x


def _in_kernel(x_ref, xprev_ref, xnext_ref, mod_ref, g_ref, wd_ref, wa_ref, wz_ref, wab_ref,
               wg_ref, cw_ref, alog_ref, dtb_ref,
               qa_ref, ka_ref, va_ref, qn_ref, kn_ref, v_ref, gb_ref, z_ref, gate_ref,
               pad_ref, h_ref, *, tm):
    i = pl.program_id(1)
    n = pl.num_programs(1)
    sh1 = mod_ref[:, 0:D_MODEL]
    sc1 = mod_ref[:, D_MODEL:2 * D_MODEL]
    gain = g_ref[...]
    normed = lambda xv: (_rms(xv, gain) * (1.0 + sc1) + sh1).astype(BF16)
    h_ref[...] = normed(x_ref[...])
    h_halo = normed(jnp.concatenate([xprev_ref[...], xnext_ref[...]], axis=0))

    def deltanet_block(c):
        cols = slice(c * MXU_N, (c + 1) * MXU_N)
        w = wd_ref[:, cols]
        d_halo = jnp.dot(h_halo, w, preferred_element_type=F32)
        pad_ref[0:HALO, cols] = jnp.where(i > 0, d_halo[0:HALO], 0.0)
        pad_ref[HALO + tm:, cols] = jnp.where(i < n - 1, d_halo[HALO:], 0.0)
        pad_ref[HALO:HALO + tm, cols] = jnp.dot(h_ref[...], w, preferred_element_type=F32)
        first = HALO - CONV_W // 2
        acc = pad_ref[first:first + tm, cols] * cw_ref[0:1, cols]
        for t in range(1, CONV_W):
            acc = acc + pad_ref[first + t:first + t + tm, cols] * cw_ref[t:t + 1, cols]
        y = _silu(acc)
        out_ref, scale = ((qn_ref, DN_DK ** -0.5), (kn_ref, None), (v_ref, None))[c * MXU_N // DN_QK_W]
        base = c * MXU_N % DN_QK_W
        for hd in range(MXU_N // DN_DK):
            yh = y[:, hd * DN_DK:(hd + 1) * DN_DK]
            if out_ref is not v_ref:
                yh = yh * lax.rsqrt(jnp.sum(yh * yh, axis=-1, keepdims=True) + EPS)
            if scale is not None:
                yh = yh * scale
            out_ref[:, base + hd * DN_DK:base + (hd + 1) * DN_DK] = yh

    def plain_block(w_ref, out_ref, c):
        cols = slice(c * MXU_N, (c + 1) * MXU_N)
        r = jnp.dot(h_ref[...], w_ref[:, cols], preferred_element_type=F32)
        out_ref[:, cols] = r.astype(BF16)

    others = [functools.partial(plain_block, w_ref, out_ref, c)
              for w_ref, out_ref, width in ((wa_ref, qa_ref, NA_W),
                                            (wa_ref.at[:, NA_W:2 * NA_W], ka_ref, NA_W),
                                            (wa_ref.at[:, 2 * NA_W:], va_ref, NA_W),
                                            (wz_ref, z_ref, DN_W), (wg_ref, gate_ref, N_GATE))
              for c in range(width // MXU_N)]
    ab = jnp.dot(h_ref[...], wab_ref[...], preferred_element_type=F32)
    n_dn = DN_QKV_W // MXU_N
    per_dn = 1
    for c in range(n_dn):
        deltanet_block(c)
        for emit in others[c * per_dn:(c + 1) * per_dn]:
            emit()
    for emit in others[n_dn * per_dn:]:
        emit()

    beta = _sigmoid(ab)
    a = ab + dtb_ref[...]
    softplus = jnp.maximum(a, 0.0) + jnp.log1p(jnp.exp(-jnp.abs(a)))
    g = -jnp.exp(alog_ref[...]) * softplus
    col = lax.broadcasted_iota(jnp.int32, ab.shape, 1)
    raw = jnp.where(col < GB_G, beta, jnp.where(col < GB_CUM, g, 0.0))
    prefix = _chunk_scan(raw, reverse=False)
    suffix = _chunk_scan(raw, reverse=True)
    total = pltpu.roll(prefix + suffix - raw, GB_TOT - GB_G, axis=1)
    cum = jnp.where(col < GB_CUM + DN_HEADS, pltpu.roll(prefix, GB_CUM - GB_G, axis=1),
                    pltpu.roll(suffix, GB_CUM - GB_G, axis=1))
    gb_ref[...] = jnp.where(col < GB_CUM, raw,
                            jnp.where(col < GB_TOT, cum, jnp.where(col < GB_END, total, 0.0)))


def _in_projection(x, mod_l, norm_g, w_parts, conv_w, a_log, dt_bias):
    nb, seq, _ = x.shape
    tm = TM_IN
    per = tm // HALO
    n_halo = seq // HALO
    wa, wd, wz, wab, wg = w_parts
    pad_cols = lambda t: jnp.pad(t.reshape(1, 2 * DN_HEADS).astype(F32),
                                 ((0, 0), (GB_G, LANES - GB_CUM)))
    tok = lambda width: pl.BlockSpec((None, tm, width), lambda b, i: (b, i, 0))
    full = lambda w: _resident(w.shape, lambda b, i: (0, 0))
    vec = lambda width: _resident((1, width), lambda b, i: (0, 0))
    out_shape = (
        jax.ShapeDtypeStruct((nb, seq, NA_W), BF16),
        jax.ShapeDtypeStruct((nb, seq, NA_W), BF16),
        jax.ShapeDtypeStruct((nb, seq, NA_W), BF16),
        jax.ShapeDtypeStruct((nb, seq, DN_QK_W), F32),
        jax.ShapeDtypeStruct((nb, seq, DN_QK_W), F32),
        jax.ShapeDtypeStruct((nb, seq, DN_W), F32),
        jax.ShapeDtypeStruct((nb, seq, LANES), F32),
        jax.ShapeDtypeStruct((nb, seq, DN_W), BF16),
        jax.ShapeDtypeStruct((nb, seq, N_GATE), BF16),
    )
    return pl.pallas_call(
        functools.partial(_in_kernel, tm=tm),
        grid=(nb, seq // tm),
        in_specs=[
            tok(D_MODEL),
            pl.BlockSpec((None, HALO, D_MODEL), lambda b, i: (b, jnp.maximum(i * per - 1, 0), 0)),
            pl.BlockSpec((None, HALO, D_MODEL),
                         lambda b, i: (b, jnp.minimum((i + 1) * per, n_halo - 1), 0)),
            pl.BlockSpec((None, 1, N_MOD * D_MODEL), lambda b, i: (b, 0, 0)),
            vec(D_MODEL),
            full(wd), full(wa), full(wz), full(wab), full(wg),
            _resident((CONV_W, DN_QKV_W), lambda b, i: (0, 0)),
            vec(LANES), vec(LANES),
        ],
        out_specs=(tok(NA_W), tok(NA_W), tok(NA_W), tok(DN_QK_W), tok(DN_QK_W), tok(DN_W),
                   tok(LANES), tok(DN_W), tok(N_GATE)),
        out_shape=out_shape,
        scratch_shapes=[pltpu.VMEM((tm + 2 * HALO, DN_QKV_W), F32),
                        pltpu.VMEM((tm, D_MODEL), BF16)],
        compiler_params=_compiler_params(("parallel", "parallel")),
        name="in_projection",
    )(x, x, x, mod_l, norm_g, wd, wa, wz, wab, wg, conv_w, pad_cols(a_log), pad_cols(dt_bias))


def _att_kernel(q_ref, k_ref, v_ref, bias_ref, o_ref, s_ref, *, n_blocks):
    j = pl.program_id(1)
    rows = n_blocks * ATT_ROWS
    base = jnp.clip(j - 1, 0, n_blocks - 3)
    lane = lax.broadcasted_iota(jnp.int32, (GRID_W, LANES), 1)
    low = lane < NA_DH

    def rows_body(it, carry):
        probs = []
        for u in range(ATT_UNROLL):
            ri = it * ATT_UNROLL + u
            r = j * ATT_ROWS + ri
            sr = jnp.clip(r - NA_KH // 2, 0, rows - NA_KH)
            delta = r - sr
            off = pl.multiple_of((sr - base * ATT_ROWS) * GRID_W, GRID_W)
            qrow = pl.multiple_of(ri * GRID_W, GRID_W)
            for p in range(NA_HEADS // 2):
                lanes = slice(p * LANES, (p + 1) * LANES)
                qp = q_ref[pl.ds(qrow, GRID_W), lanes]
                zero = jnp.zeros_like(qp)
                lhs = jnp.concatenate([jnp.where(low, qp, zero), jnp.where(low, zero, qp)], axis=0)
                kp = k_ref[0, pl.ds(off, NA_KH * GRID_W), lanes]
                s_ref[len(probs)] = lax.dot_general(lhs, kp, (((1,), (1,)), ((), ())),
                                                    preferred_element_type=F32)
                probs.append((qrow, off, delta, p, lanes, len(probs)))
        pvs = []
        for qrow, off, delta, p, lanes, slot in probs:
            s = s_ref[slot] + bias_ref[delta, p]
            m = jnp.max(s, axis=-1, keepdims=True)
            e = jnp.exp(s - m)
            den = jnp.sum(e, axis=-1, keepdims=True)
            vp = v_ref[0, pl.ds(off, NA_KH * GRID_W), lanes]
            pvs.append((jnp.dot(e.astype(BF16), vp, preferred_element_type=F32), den))
        for (qrow, off, delta, p, lanes, slot), (pv, den) in zip(probs, pvs):
            pv = pv / den
            o_ref[pl.ds(qrow, GRID_W), lanes] = jnp.where(low, pv[0:GRID_W], pv[GRID_W:]).astype(BF16)
        return carry

    lax.fori_loop(0, ATT_ROWS // ATT_UNROLL, rows_body, 0)


def _attention(qa, ka, va, bias):
    nb, seq, _ = qa.shape
    blk = ATT_ROWS * GRID_W
    n_blocks = seq // blk
    assert n_blocks >= 3
    win = ATT_WIN_ROWS * GRID_W

    def win_map(b, j):
        return (b, jnp.clip(j - 1, 0, n_blocks - 3) * blk, 0)

    win_spec = pl.BlockSpec((pl.Element(1), pl.Element(win), pl.Element(NA_W)), win_map)
    return pl.pallas_call(
        functools.partial(_att_kernel, n_blocks=n_blocks),
        grid=(nb, n_blocks),
        in_specs=[
            pl.BlockSpec((None, blk, NA_W), lambda b, j: (b, j, 0)),
            win_spec, win_spec,
            _resident(bias.shape, lambda b, j: (0, 0, 0, 0)),
        ],
        out_specs=pl.BlockSpec((None, blk, NA_W), lambda b, j: (b, j, 0)),
        out_shape=jax.ShapeDtypeStruct((nb, seq, NA_W), BF16),
        scratch_shapes=[pltpu.VMEM((ATT_UNROLL * NA_HEADS // 2, 2 * GRID_W, NA_KH * GRID_W), F32)],
        compiler_params=_compiler_params(("parallel", "parallel")),
        name="neighborhood_attention",
    )(qa, ka, va, bias)


def _attention_bias(rpb):
    qc = np.arange(GRID_W)
    kc = np.arange(GRID_W)
    start = np.clip(qc - NA_KW // 2, 0, GRID_W - NA_KW)
    valid = (kc[None, :] >= start[:, None]) & (kc[None, :] < start[:, None] + NA_KW)
    neg = jnp.asarray(np.where(valid, 0.0, -1e30).astype(np.float32))
    dc = np.clip(kc[None, :] - qc[:, None] + NA_KW - 1, 0, 2 * NA_KW - 2)
    delta = np.arange(NA_KH)
    dr = np.arange(NA_KH)[None, :] - delta[:, None] + (NA_KH - 1)
    onehot = jnp.asarray((dc[None] == np.arange(2 * NA_KW - 1)[:, None, None]).astype(np.float32))
    t = jnp.einsum('hdic,cqk->dhqik', rpb.astype(F32)[:, dr], onehot, precision=HIGHEST)
    t = t + neg[None, None, :, None, :]
    return t.reshape(NA_KH, NA_HEADS // 2, 2 * GRID_W, NA_KH * GRID_W)


_LOG2_CHUNK = CHUNK.bit_length() - 1


def _dn_prepare(refs, stage, slot):
    row = lax.broadcasted_iota(jnp.int32, (CHUNK, CHUNK), 0)
    colm = lax.broadcasted_iota(jnp.int32, (CHUNK, CHUNK), 1)
    nt = (((1,), (1,)), ((), ()))
    probs = []
    for reverse in (False, True):
        q_ref, k_ref, v_ref, g_ref = refs[reverse]
        incl, strict = (colm >= row, colm > row) if reverse else (colm <= row, colm < row)
        for c in range(DN_GROUP):
            rows = slice(c * CHUNK, (c + 1) * CHUNK)
            gb = g_ref[rows, :]
            gb_t = gb.T
            for h in range(DN_HEADS):
                lanes = slice(h * DN_DK, (h + 1) * DN_DK)
                hd = h + (DN_HEADS if reverse else 0)
                q, k, v = q_ref[rows, lanes], k_ref[rows, lanes], v_ref[rows, lanes]
                q_b, k_b = q.astype(BF16), k.astype(BF16)
                probs.append(dict(
                    reverse=reverse, incl=incl, strict=strict, q=q, k=k, v=v,
                    beta=gb[:, GB_BETA + hd:GB_BETA + hd + 1],
                    g_col=gb[:, GB_CUM + hd:GB_CUM + hd + 1],
                    g_row=gb_t[GB_CUM + hd:GB_CUM + hd + 1, :],
                    g_tot_row=gb_t[GB_TOT + hd:GB_TOT + hd + 1, :],
                    qk_kk=lax.dot_general(jnp.concatenate([q_b, k_b], axis=0), k_b, nt,
                                          preferred_element_type=F32)))
    yield
    for i, p in enumerate(probs):
        beta = jnp.broadcast_to(p["beta"], (CHUNK, DN_DK))
        g_col = jnp.broadcast_to(p["g_col"], (CHUNK, DN_DK))
        g_tot = jnp.broadcast_to(p["g_tot_row"], (CHUNK, CHUNK))
        g_tot = jnp.concatenate([g_tot, g_tot], axis=1)
        decay = jnp.exp(jnp.where(p["incl"], g_col[:, :CHUNK] - p["g_row"], -jnp.inf))
        stage["qk"][slot, i] = (p["qk_kk"][:CHUNK] * decay).astype(BF16)
        p["a"] = jnp.where(p["strict"], beta[:, :CHUNK] * p["qk_kk"][CHUNK:] * decay, 0.0)
        exp_g = jnp.exp(g_col)
        p["x"] = jnp.concatenate([(beta * exp_g) * p["k"], beta * p["v"]], axis=1)
        p["q_dec"] = p["q"] * exp_g
        k_dec = p["k"] * jnp.exp(g_tot - g_col)
        stage["kdt"][slot, i] = k_dec.T.astype(BF16)
        stage["gl"][slot, i] = jnp.exp(g_tot[0:SUBLANES])
    wide = DN_HEADS * CHUNK
    diag = ((lax.broadcasted_iota(jnp.int32, (wide, wide), 0) >> _LOG2_CHUNK)
            == (lax.broadcasted_iota(jnp.int32, (wide, wide), 1) >> _LOG2_CHUNK))

    def block_diag(m):
        m = jnp.concatenate([m] * DN_HEADS, axis=0)
        return jnp.where(diag, m, 0.0).astype(BF16)

    ri = lax.broadcasted_iota(jnp.int32, (CHUNK, wide), 0)
    ci = lax.broadcasted_iota(jnp.int32, (CHUNK, wide), 1) & (CHUNK - 1)
    groups = []
    for g0 in range(0, len(probs), DN_HEADS):
        members = probs[g0:g0 + DN_HEADS]
        groups.append(dict(members=members, first=g0, reverse=members[0]["reverse"],
                           a=jnp.concatenate([p["a"] for p in members], axis=1)))
    for level in range(_LOG2_CHUNK):
        for g in groups:
            lo_i, lo_j = ((ri >> level) & 1) == 1, ((ci >> level) & 1) == 0
            couple = ((ri >> (level + 1)) == (ci >> (level + 1))) & (
                (~lo_i & ~lo_j) if g["reverse"] else (lo_i & lo_j))
            g["l"] = jnp.where(couple, g["a"], 0.0)
        if level == 0:
            for g in groups:
                g["e"] = -g["l"]
            continue
        prods = [jnp.dot(g["l"].astype(BF16), block_diag(g["e"]), preferred_element_type=F32)
                 for g in groups]
        yield
        for g, r in zip(groups, prods):
            g["m"] = g["l"] + r
        prods = [jnp.dot(g["e"].astype(BF16), block_diag(g["m"]), preferred_element_type=F32)
                 for g in groups]
        yield
        for g, r in zip(groups, prods):
            g["e"] = g["e"] - (g["m"] + r)
    low = lax.broadcasted_iota(jnp.int32, (CHUNK, 2 * CHUNK), 1) < CHUNK
    prods = []
    for g in groups:
        for pair in range(DN_HEADS // 2):
            e = g["e"][:, pair * 2 * CHUNK:(pair + 1) * 2 * CHUNK]
            lhs = jnp.concatenate([jnp.where(low, e, 0.0), jnp.where(low, 0.0, e)], axis=0)
            rhs = jnp.concatenate([p["x"] for p in g["members"][2 * pair:2 * pair + 2]], axis=0)
            prods.append(jnp.dot(lhs.astype(BF16), rhs.astype(BF16), preferred_element_type=F32))
    yield
    for i, p in enumerate(probs):
        r = prods[i // 2]
        x = p["x"] + (r[CHUNK:] if i % 2 else r[:CHUNK])
        stage["wq"][slot, i] = jnp.concatenate([x[:, :DN_DK], p["q_dec"]], axis=0).astype(BF16)
        stage["u"][slot, i] = x[:, DN_DK:]


def _dn_scan(out_refs, state_refs, stage, slot):
    states = {(rev, h): state_refs[rev][h] for rev in (False, True) for h in range(DN_HEADS)}
    for step in range(DN_GROUP):
        chains = []
        for reverse in (False, True):
            c = DN_GROUP - 1 - step if reverse else step
            for h in range(DN_HEADS):
                chains.append((reverse, c, h, ((DN_GROUP if reverse else 0) + c) * DN_HEADS + h))
        ws_qs = [jnp.dot(stage["wq"][slot, i], states[rev, h].astype(BF16),
                         preferred_element_type=F32) for rev, c, h, i in chains]
        yield
        v_new = [(stage["u"][slot, i] - r[:CHUNK]).astype(BF16)
                 for (rev, c, h, i), r in zip(chains, ws_qs)]
        intra = [jnp.dot(stage["qk"][slot, i], vn, preferred_element_type=F32)
                 for (rev, c, h, i), vn in zip(chains, v_new)]
        ds = [jnp.dot(stage["kdt"][slot, i], vn, preferred_element_type=F32)
              for (rev, c, h, i), vn in zip(chains, v_new)]
        yield
        for (rev, c, h, i), r, oi, d in zip(chains, ws_qs, intra, ds):
            out_refs[rev][c * CHUNK:(c + 1) * CHUNK, h * DN_DV:(h + 1) * DN_DV] = r[CHUNK:] + oi
            g_last = jnp.concatenate([stage["gl"][slot, i]] * (DN_DK // SUBLANES), axis=0)
            states[rev, h] = states[rev, h] * g_last + d
    for (rev, h), s in states.items():
        state_refs[rev][h] = s


def _dn_kernel(qf_ref, kf_ref, vf_ref, gf_ref, qb_ref, kb_ref, vb_ref, gbb_ref,
               of_ref, ob_ref, sf_ref, sb_ref, wq_ref, u_ref, qk_ref, kdt_ref, gl_ref, *, n):
    t = pl.program_id(0)
    stage = dict(wq=wq_ref, u=u_ref, qk=qk_ref, kdt=kdt_ref, gl=gl_ref)

    @pl.when(t == 0)
    def _():
        for ref in stage.values():
            ref[...] = jnp.zeros_like(ref)

    @pl.when((t == 0) | (lax.rem(t + n - 1, n) == 0))
    def _():
        sf_ref[...] = jnp.zeros_like(sf_ref)
        sb_ref[...] = jnp.zeros_like(sb_ref)

    slot = lax.rem(t, 2)
    prepare = _dn_prepare({False: (qf_ref, kf_ref, vf_ref, gf_ref),
                           True: (qb_ref, kb_ref, vb_ref, gbb_ref)}, stage, slot)
    scan = _dn_scan({False: of_ref, True: ob_ref}, {False: sf_ref, True: sb_ref}, stage, 1 - slot)
    live = [prepare, scan]
    while live:
        for gen in list(live):
            if next(gen, StopIteration) is StopIteration:
                live.remove(gen)


def _delta_rule(qn, kn, v, gb):
    nb, seq, _ = qn.shape
    tg = DN_GROUP * CHUNK
    n = seq // tg
    total = nb * n
    n_prob = 2 * DN_GROUP * DN_HEADS

    def in_map(reverse):
        def index(t):
            tt = jnp.minimum(t, total - 1)
            i = lax.rem(tt, n)
            return (tt // n, n - 1 - i if reverse else i, 0)
        return index

    def out_map(reverse):
        def index(t):
            tt = jnp.maximum(t - 1, 0)
            i = lax.rem(tt, n)
            return (tt // n, n - 1 - i if reverse else i, 0)
        return index

    spec = lambda width, index: pl.BlockSpec((None, tg, width), index)
    widths = (DN_QK_W, DN_QK_W, DN_W, LANES)
    state = pltpu.VMEM((DN_HEADS, DN_DK, DN_DV), F32)
    return pl.pallas_call(
        functools.partial(_dn_kernel, n=n),
        grid=(total + 1,),
        in_specs=[spec(w, in_map(False)) for w in widths] + [spec(w, in_map(True)) for w in widths],
        out_specs=(spec(DN_W, out_map(False)), spec(DN_W, out_map(True))),
        out_shape=(jax.ShapeDtypeStruct((nb, seq, DN_W), F32),
                   jax.ShapeDtypeStruct((nb, seq, DN_W), F32)),
        scratch_shapes=[
            state, state,
            pltpu.VMEM((2, n_prob, DN_DK, DN_DK), BF16),
            pltpu.VMEM((2, n_prob, CHUNK, DN_DV), F32),
            pltpu.VMEM((2, n_prob, CHUNK, CHUNK), BF16),
            pltpu.VMEM((2, n_prob, DN_DK, CHUNK), BF16),
            pltpu.VMEM((2, n_prob, SUBLANES, LANES), F32),
        ],
        compiler_params=_compiler_params(("arbitrary",)),
        name="delta_rule",
    )(qn, kn, v, gb, qn, kn, v, gb)


def _post_kernel(x_ref, oa_ref, of_ref, ob_ref, z_ref, gate_ref, mod_ref, dng_ref, g2_ref,
                 fg_ref, wba_ref, wbd_ref, wo_ref, w1_ref, w2_ref, y_ref, *, final):
    od = of_ref[...] + ob_ref[...]
    z = z_ref[...].astype(F32)
    dng = dng_ref[...]
    parts = []
    for h in range(DN_HEADS):
        lanes = slice(h * DN_DV, (h + 1) * DN_DV)
        parts.append(_rms(od[:, lanes], dng) * _silu(z[:, lanes]))
    od_n = jnp.concatenate(parts, axis=1).astype(BF16)
    br_a = jnp.dot(oa_ref[...], wba_ref[...], preferred_element_type=F32)
    br_d = jnp.dot(od_n, wbd_ref[...], preferred_element_type=F32)
    gate_a = _sigmoid(gate_ref[:, 0:D_MODEL].astype(F32))
    gate_d = _sigmoid(gate_ref[:, D_MODEL:].astype(F32))
    merged = gate_a * br_a + gate_d * br_d
    mix = jnp.dot(merged.astype(BF16), wo_ref[...], preferred_element_type=F32)
    gt1 = mod_ref[:, 2 * D_MODEL:3 * D_MODEL]
    sh2 = mod_ref[:, 3 * D_MODEL:4 * D_MODEL]
    sc2 = mod_ref[:, 4 * D_MODEL:5 * D_MODEL]
    gt2 = mod_ref[:, 5 * D_MODEL:6 * D_MODEL]
    x = x_ref[...] + gt1 * mix
    h2 = (_rms(x, g2_ref[...]) * (1.0 + sc2) + sh2).astype(BF16)
    acc = jnp.zeros_like(x)
    for j in range(D_FF // D_MODEL):
        cols = slice(j * D_MODEL, (j + 1) * D_MODEL)
        hid = jnp.maximum(jnp.dot(h2, w1_ref[:, cols], preferred_element_type=F32), 0.0)
        acc = acc + jnp.dot((hid * hid).astype(BF16), w2_ref[cols, :], preferred_element_type=F32)
    x = x + gt2 * acc
    if final:
        x = _rms(x, fg_ref[...])
    y_ref[...] = x


def _post(x, oa, of, ob, z, gates, mod_l, dn_norm_g, norm_mlp_g, final_g, weights, final):
    nb, seq, _ = x.shape
    tm = TM_POST
    tok = lambda width: pl.BlockSpec((None, tm, width), lambda b, i: (b, i, 0))
    vec = lambda width: _resident((1, width), lambda b, i: (0, 0))
    full = lambda w: _resident(w.shape, lambda b, i: (0, 0))
    return pl.pallas_call(
        functools.partial(_post_kernel, final=final),
        grid=(nb, seq // tm),
        in_specs=[
            tok(D_MODEL), tok(NA_W), tok(DN_W), tok(DN_W), tok(DN_W), tok(N_GATE),
            pl.BlockSpec((None, 1, N_MOD * D_MODEL), lambda b, i: (b, 0, 0)),
            vec(DN_DV), vec(D_MODEL), vec(D_MODEL),
        ] + [full(w) for w in weights],
        out_specs=tok(D_MODEL),
        out_shape=jax.ShapeDtypeStruct((nb, seq, D_MODEL), F32),
        compiler_params=_compiler_params(("parallel", "parallel")),
        name="merge_mlp",
    )(x, oa, of, ob, z, gates, mod_l, dn_norm_g, norm_mlp_g, final_g, *weights)


def _split_w_in(w_in_l):
    o1 = 3 * NA_W
    o2 = o1 + DN_QKV_W
    o3 = o2 + DN_W
    o4 = o3 + AB_W
    wab = jnp.pad(w_in_l[:, o3:o4], ((0, 0), (0, LANES - AB_W)))
    wa = jnp.concatenate([w_in_l[:, :NA_W] * (NA_DH ** -0.5), w_in_l[:, NA_W:o1]], axis=1)
    parts = (wa, w_in_l[:, o1:o2], w_in_l[:, o2:o3], wab, w_in_l[:, o4:])
    return tuple(p.astype(BF16) for p in parts)


def _trunk(x, mod, p):
    for l in range(DEPTH):
        qa, ka, va, qn, kn, v, gb, z, gates = _in_projection(
            x, mod[l], p["norm_mix_g"][l], p["w_in"][l], p["dn_conv"][l], p["dn_a_log"][l],
            p["dn_dt_bias"][l])
        oa = _attention(qa, ka, va, p["bias"][l])
        of, ob = _delta_rule(qn, kn, v, gb)
        x = _post(x, oa, of, ob, z, gates, mod[l], p["dn_norm_g"][l], p["norm_mlp_g"][l],
                  p["final_norm_g"], p["post_w"][l], final=(l == DEPTH - 1))
    return x


def kernel(x_prompt, x_sample, c_prompt, c_sample, norm_mix_g, norm_mlp_g, w_ada, b_ada, w_in,
           na_rpb, dn_conv, dn_a_log, dn_dt_bias, dn_norm_g, w_br_attn, w_br_dn, w_out, w_mlp1,
           w_mlp2, final_norm_g):
    row = lambda t: t.reshape(1, -1).astype(F32)
    p = {
        "norm_mix_g": [row(norm_mix_g[l]) for l in range(DEPTH)],
        "norm_mlp_g": [row(norm_mlp_g[l]) for l in range(DEPTH)],
        "dn_norm_g": [row(dn_norm_g[l]) for l in range(DEPTH)],
        "final_norm_g": row(final_norm_g),
        "w_in": [_split_w_in(w_in[l]) for l in range(DEPTH)],
        "bias": [_attention_bias(na_rpb[l]) for l in range(DEPTH)],
        "dn_conv": [dn_conv[l].astype(F32) for l in range(DEPTH)],
        "dn_a_log": dn_a_log,
        "dn_dt_bias": dn_dt_bias,
        "post_w": [tuple(w[l].astype(BF16) for w in (w_br_attn, w_br_dn, w_out, w_mlp1, w_mlp2))
                   for l in range(DEPTH)],
    }
    n_prompt = c_prompt.shape[0]
    mod = _modulation(jnp.concatenate([c_prompt, c_sample], axis=0), w_ada, b_ada)
    mod = mod.reshape(DEPTH, -1, 1, N_MOD * D_MODEL)
    y_prompt = _trunk(x_prompt, mod[:, :n_prompt], p)
    y_sample = _trunk(x_sample, mod[:, n_prompt:], p)
    return (y_prompt, y_sample)
```

```python
import functools

import jax
import jax.numpy as jnp
import numpy as np
from jax import lax
from jax.experimental import pallas as pl
from jax.experimental.pallas import tpu as pltpu

D_MODEL = 1024
DEPTH = 2
GRID_W = 64
NA_HEADS = 8
NA_DH = 64
NA_W = NA_HEADS * NA_DH
NA_KH = 8
NA_KW = 16
DN_HEADS = 4
DN_DK = 128
DN_DV = 128
DN_QK_W = DN_HEADS * DN_DK
DN_W = DN_HEADS * DN_DV
DN_QKV_W = 2 * DN_QK_W + DN_W
CONV_W = 5
CHUNK = 64
D_FF = 4 * D_MODEL
N_MOD = 6
EPS = 1e-6
N_GATE = 2 * D_MODEL
AB_W = 4 * DN_HEADS
GB_BETA = 0
GB_G = 2 * DN_HEADS
GB_CUM = 4 * DN_HEADS
GB_TOT = 6 * DN_HEADS
GB_END = 8 * DN_HEADS

LANES = 128
SUBLANES = 8
MXU_N = 256
VMEM_LIMIT_BYTES = 56 * 1024 * 1024

TM_IN = 512
TM_POST = 512
ATT_ROWS = 8
ATT_WIN_ROWS = 3 * ATT_ROWS
ATT_UNROLL = 8
DN_GROUP = 8
HALO = SUBLANES

BF16 = jnp.bfloat16
F32 = jnp.float32
HIGHEST = lax.Precision.HIGHEST


def _compiler_params(semantics):
    return pltpu.CompilerParams(dimension_semantics=semantics,
                                vmem_limit_bytes=VMEM_LIMIT_BYTES)


def _resident(block_shape, index_map):
    return pl.BlockSpec(block_shape, index_map, pipeline_mode=pl.Buffered(1))


def _silu(x):
    half = 0.5 * x
    return half * (jnp.tanh(half) + 1.0)


def _sigmoid(x):
    return 0.5 * (jnp.tanh(0.5 * x) + 1.0)


def _rms(x, g):
    return (x * lax.rsqrt(jnp.mean(x * x, axis=-1, keepdims=True) + EPS)) * g


def _mod_kernel(c_ref, w_ref, b_ref, o_ref):
    c = _silu(c_ref[...])
    o_ref[...] = jnp.dot(c, w_ref[...], precision=HIGHEST,
                         preferred_element_type=F32) + b_ref[...]


def _modulation(c, w_ada, b_ada):
    nb = c.shape[0]
    tn = 1024
    n_out = N_MOD * D_MODEL
    return pl.pallas_call(
        _mod_kernel,
        grid=(DEPTH, n_out // tn),
        in_specs=[
            pl.BlockSpec((nb, D_MODEL), lambda l, j: (0, 0)),
            pl.BlockSpec((None, D_MODEL, tn), lambda l, j: (l, 0, j)),
            pl.BlockSpec((None, 1, tn), lambda l, j: (l, 0, j)),
        ],
        out_specs=pl.BlockSpec((None, nb, tn), lambda l, j: (l, 0, j)),
        out_shape=jax.ShapeDtypeStruct((DEPTH, nb, n_out), F32),
        compiler_params=_compiler_params(("arbitrary", "arbitrary")),
        name="adaln_mod",
    )(c, w_ada, b_ada.reshape(DEPTH, 1, n_out))


def _chunk_scan(x, reverse):
    tm = x.shape[0]
    pos = lax.broadcasted_iota(jnp.int32, x.shape, 0) & (CHUNK - 1)
    step = 1
    while step < CHUNK:
        if reverse:
            shifted = pltpu.roll(x, tm - step, axis=0)
            x = x + jnp.where(pos < CHUNK - step, shifted, 0.0)
        else:
            shifted = pltpu.roll(x, step, axis=0)
            x = x + jnp.where(pos >= step, shifted, 0.0)
        step *= 2
    return x


def _in_kernel(x_ref, xprev_ref, xnext_ref, mod_ref, g_ref, wd_ref, wa_ref, wz_ref, wab_ref,
               wg_ref, cw_ref, alog_ref, dtb_ref,
               qa_ref, ka_ref, va_ref, qn_ref, kn_ref, v_ref, gb_ref, z_ref, gate_ref,
               pad_ref, h_ref, *, tm):
    i = pl.program_id(1)
    n = pl.num_programs(1)
    sh1 = mod_ref[:, 0:D_MODEL]
    sc1 = mod_ref[:, D_MODEL:2 * D_MODEL]
    gain = g_ref[...]
    normed = lambda xv: (_rms(xv, gain) * (1.0 + sc1) + sh1).astype(BF16)
    h_ref[...] = normed(x_ref[...])
    h_halo = normed(jnp.concatenate([xprev_ref[...], xnext_ref[...]], axis=0))

    def deltanet_block(c):
        cols = slice(c * MXU_N, (c + 1) * MXU_N)
        w = wd_ref[:, cols]
        d_halo = jnp.dot(h_halo, w, preferred_element_type=F32)
        pad_ref[0:HALO, cols] = jnp.where(i > 0, d_halo[0:HALO], 0.0)
        pad_ref[HALO + tm:, cols] = jnp.where(i < n - 1, d_halo[HALO:], 0.0)
        pad_ref[HALO:HALO + tm, cols] = jnp.dot(h_ref[...], w, preferred_element_type=F32)
        first = HALO - CONV_W // 2
        acc = pad_ref[first:first + tm, cols] * cw_ref[0:1, cols]
        for t in range(1, CONV_W):
            acc = acc + pad_ref[first + t:first + t + tm, cols] * cw_ref[t:t + 1, cols]
        y = _silu(acc)
        out_ref, scale = ((qn_ref, DN_DK ** -0.5), (kn_ref, None), (v_ref, None))[c * MXU_N // DN_QK_W]
        base = c * MXU_N % DN_QK_W
        for hd in range(MXU_N // DN_DK):
            yh = y[:, hd * DN_DK:(hd + 1) * DN_DK]
            if out_ref is not v_ref:
                yh = yh * lax.rsqrt(jnp.sum(yh * yh, axis=-1, keepdims=True) + EPS)
            if scale is not None:
                yh = yh * scale
            out_ref[:, base + hd * DN_DK:base + (hd + 1) * DN_DK] = yh.astype(out_ref.dtype)

    def plain_block(w_ref, out_ref, c):
        cols = slice(c * MXU_N, (c + 1) * MXU_N)
        r = jnp.dot(h_ref[...], w_ref[:, cols], preferred_element_type=F32)
        out_ref[:, cols] = r.astype(BF16)

    others = [functools.partial(plain_block, w_ref, out_ref, c)
              for w_ref, out_ref, width in ((wa_ref, qa_ref, NA_W),
                                            (wa_ref.at[:, NA_W:2 * NA_W], ka_ref, NA_W),
                                            (wa_ref.at[:, 2 * NA_W:], va_ref, NA_W),
                                            (wz_ref, z_ref, DN_W), (wg_ref, gate_ref, N_GATE))
              for c in range(width // MXU_N)]
    ab = jnp.dot(h_ref[...], wab_ref[...], preferred_element_type=F32)
    n_dn = DN_QKV_W // MXU_N
    per_dn = 1
    for c in range(n_dn):
        deltanet_block(c)
        for emit in others[c * per_dn:(c + 1) * per_dn]:
            emit()
    for emit in others[n_dn * per_dn:]:
        emit()

    beta = _sigmoid(ab)
    a = ab + dtb_ref[...]
    softplus = jnp.maximum(a, 0.0) + jnp.log1p(jnp.exp(-jnp.abs(a)))
    g = -jnp.exp(alog_ref[...]) * softplus
    col = lax.broadcasted_iota(jnp.int32, ab.shape, 1)
    raw = jnp.where(col < GB_G, beta, jnp.where(col < GB_CUM, g, 0.0))
    prefix = _chunk_scan(raw, reverse=False)
    suffix = _chunk_scan(raw, reverse=True)
    total = pltpu.roll(prefix + suffix - raw, GB_TOT - GB_G, axis=1)
    cum = jnp.where(col < GB_CUM + DN_HEADS, pltpu.roll(prefix, GB_CUM - GB_G, axis=1),
                    pltpu.roll(suffix, GB_CUM - GB_G, axis=1))
    gb_ref[...] = jnp.where(col < GB_CUM, raw,
                            jnp.where(col < GB_TOT, cum, jnp.where(col < GB_END, total, 0.0)))


def _in_projection(x, mod_l, norm_g, w_parts, conv_w, a_log, dt_bias):
    nb, seq, _ = x.shape
    tm = TM_IN
    per = tm // HALO
    n_halo = seq // HALO
    wa, wd, wz, wab, wg = w_parts
    pad_cols = lambda t: jnp.pad(t.reshape(1, 2 * DN_HEADS).astype(F32),
                                 ((0, 0), (GB_G, LANES - GB_CUM)))
    tok = lambda width: pl.BlockSpec((None, tm, width), lambda b, i: (b, i, 0))
    full = lambda w: _resident(w.shape, lambda b, i: (0, 0))
    vec = lambda width: _resident((1, width), lambda b, i: (0, 0))
    out_shape = (
        jax.ShapeDtypeStruct((nb, seq, NA_W), BF16),
        jax.ShapeDtypeStruct((nb, seq, NA_W), BF16),
        jax.ShapeDtypeStruct((nb, seq, NA_W), BF16),
        jax.ShapeDtypeStruct((nb, seq, DN_QK_W), BF16),
        jax.ShapeDtypeStruct((nb, seq, DN_QK_W), BF16),
        jax.ShapeDtypeStruct((nb, seq, DN_W), BF16),
        jax.ShapeDtypeStruct((nb, seq, LANES), F32),
        jax.ShapeDtypeStruct((nb, seq, DN_W), BF16),
        jax.ShapeDtypeStruct((nb, seq, N_GATE), BF16),
    )
    return pl.pallas_call(
        functools.partial(_in_kernel, tm=tm),
        grid=(nb, seq // tm),
        in_specs=[
            tok(D_MODEL),
            pl.BlockSpec((None, HALO, D_MODEL), lambda b, i: (b, jnp.maximum(i * per - 1, 0), 0)),
            pl.BlockSpec((None, HALO, D_MODEL),
                         lambda b, i: (b, jnp.minimum((i + 1) * per, n_halo - 1), 0)),
            pl.BlockSpec((None, 1, N_MOD * D_MODEL), lambda b, i: (b, 0, 0)),
            vec(D_MODEL),
            full(wd), full(wa), full(wz), full(wab), full(wg),
            _resident((CONV_W, DN_QKV_W), lambda b, i: (0, 0)),
            vec(LANES), vec(LANES),
        ],
        out_specs=(tok(NA_W), tok(NA_W), tok(NA_W), tok(DN_QK_W), tok(DN_QK_W), tok(DN_W),
                   tok(LANES), tok(DN_W), tok(N_GATE)),
        out_shape=out_shape,
        scratch_shapes=[pltpu.VMEM((tm + 2 * HALO, DN_QKV_W), F32),
                        pltpu.VMEM((tm, D_MODEL), BF16)],
        compiler_params=_compiler_params(("parallel", "parallel")),
        name="in_projection",
    )(x, x, x, mod_l, norm_g, wd, wa, wz, wab, wg, conv_w, pad_cols(a_log), pad_cols(dt_bias))


def _att_kernel(q_ref, k_ref, v_ref, bias_ref, o_ref, s_ref, *, n_blocks):
    j = pl.program_id(1)
    rows = n_blocks * ATT_ROWS
    base = jnp.clip(j - 1, 0, n_blocks - 3)
    lane = lax.broadcasted_iota(jnp.int32, (GRID_W, LANES), 1)
    low = lane < NA_DH

    def rows_body(it, carry):
        probs = []
        for u in range(ATT_UNROLL):
            ri = it * ATT_UNROLL + u
            r = j * ATT_ROWS + ri
            sr = jnp.clip(r - NA_KH // 2, 0, rows - NA_KH)
            delta = r - sr
            off = pl.multiple_of((sr - base * ATT_ROWS) * GRID_W, GRID_W)
            qrow = pl.multiple_of(ri * GRID_W, GRID_W)
            for p in range(NA_HEADS // 2):
                lanes = slice(p * LANES, (p + 1) * LANES)
                qp = q_ref[pl.ds(qrow, GRID_W), lanes]
                zero = jnp.zeros_like(qp)
                lhs = jnp.concatenate([jnp.where(low, qp, zero), jnp.where(low, zero, qp)], axis=0)
                kp = k_ref[0, pl.ds(off, NA_KH * GRID_W), lanes]
                s_ref[len(probs)] = lax.dot_general(lhs, kp, (((1,), (1,)), ((), ())),
                                                    preferred_element_type=F32)
                probs.append((qrow, off, delta, p, lanes, len(probs)))
        pvs = []
        for qrow, off, delta, p, lanes, slot in probs:
            s = s_ref[slot] + bias_ref[delta, p]
            m = jnp.max(s, axis=-1, keepdims=True)
            e = jnp.exp(s - m)
            den = jnp.sum(e, axis=-1, keepdims=True)
            vp = v_ref[0, pl.ds(off, NA_KH * GRID_W), lanes]
            pvs.append((jnp.dot(e.astype(BF16), vp, preferred_element_type=F32), den))
        for (qrow, off, delta, p, lanes, slot), (pv, den) in zip(probs, pvs):
            pv = pv / den
            o_ref[pl.ds(qrow, GRID_W), lanes] = jnp.where(low, pv[0:GRID_W], pv[GRID_W:]).astype(BF16)
        return carry

    lax.fori_loop(0, ATT_ROWS // ATT_UNROLL, rows_body, 0)


def _attention(qa, ka, va, bias):
    nb, seq, _ = qa.shape
    blk = ATT_ROWS * GRID_W
    n_blocks = seq // blk
    assert n_blocks >= 3
    win = ATT_WIN_ROWS * GRID_W

    def win_map(b, j):
        return (b, jnp.clip(j - 1, 0, n_blocks - 3) * blk, 0)

    win_spec = pl.BlockSpec((pl.Element(1), pl.Element(win), pl.Element(NA_W)), win_map)
    return pl.pallas_call(
        functools.partial(_att_kernel, n_blocks=n_blocks),
        grid=(nb, n_blocks),
        in_specs=[
            pl.BlockSpec((None, blk, NA_W), lambda b, j: (b, j, 0)),
            win_spec, win_spec,
            _resident(bias.shape, lambda b, j: (0, 0, 0, 0)),
        ],
        out_specs=pl.BlockSpec((None, blk, NA_W), lambda b, j: (b, j, 0)),
        out_shape=jax.ShapeDtypeStruct((nb, seq, NA_W), BF16),
        scratch_shapes=[pltpu.VMEM((ATT_UNROLL * NA_HEADS // 2, 2 * GRID_W, NA_KH * GRID_W), F32)],
        compiler_params=_compiler_params(("parallel", "parallel")),
        name="neighborhood_attention",
    )(qa, ka, va, bias)


def _attention_bias(rpb):
    qc = np.arange(GRID_W)
    kc = np.arange(GRID_W)
    start = np.clip(qc - NA_KW // 2, 0, GRID_W - NA_KW)
    valid = (kc[None, :] >= start[:, None]) & (kc[None, :] < start[:, None] + NA_KW)
    neg = jnp.asarray(np.where(valid, 0.0, -1e30).astype(np.float32))
    dc = np.clip(kc[None, :] - qc[:, None] + NA_KW - 1, 0, 2 * NA_KW - 2)
    delta = np.arange(NA_KH)
    dr = np.arange(NA_KH)[None, :] - delta[:, None] + (NA_KH - 1)
    onehot = jnp.asarray((dc[None] == np.arange(2 * NA_KW - 1)[:, None, None]).astype(np.float32))
    t = jnp.einsum('hdic,cqk->dhqik', rpb.astype(F32)[:, dr], onehot, precision=HIGHEST)
    t = t + neg[None, None, :, None, :]
    return t.reshape(NA_KH, NA_HEADS // 2, 2 * GRID_W, NA_KH * GRID_W)


_LOG2_CHUNK = CHUNK.bit_length() - 1


def _dn_prepare(refs, stage, slot):
    row = lax.broadcasted_iota(jnp.int32, (CHUNK, CHUNK), 0)
    colm = lax.broadcasted_iota(jnp.int32, (CHUNK, CHUNK), 1)
    nt = (((1,), (1,)), ((), ()))
    probs = []
    for reverse in (False, True):
        q_ref, k_ref, v_ref, g_ref = refs[reverse]
        incl, strict = (colm >= row, colm > row) if reverse else (colm <= row, colm < row)
        for c in range(DN_GROUP):
            rows = slice(c * CHUNK, (c + 1) * CHUNK)
            gb = g_ref[rows, :]
            gb_t = gb.T
            for h in range(DN_HEADS):
                lanes = slice(h * DN_DK, (h + 1) * DN_DK)
                hd = h + (DN_HEADS if reverse else 0)
                q_b, k_b = q_ref[rows, lanes], k_ref[rows, lanes]
                q, k, v = q_b.astype(F32), k_b.astype(F32), v_ref[rows, lanes].astype(F32)
                probs.append(dict(
                    reverse=reverse, incl=incl, strict=strict, q=q, k=k, v=v,
                    beta=gb[:, GB_BETA + hd:GB_BETA + hd + 1],
                    g_col=gb[:, GB_CUM + hd:GB_CUM + hd + 1],
                    g_row=gb_t[GB_CUM + hd:GB_CUM + hd + 1, :],
                    g_tot_row=gb_t[GB_TOT + hd:GB_TOT + hd + 1, :],
                    qk_kk=lax.dot_general(jnp.concatenate([q_b, k_b], axis=0), k_b, nt,
                                          preferred_element_type=F32)))
    yield
    for i, p in enumerate(probs):
        beta = jnp.broadcast_to(p["beta"], (CHUNK, DN_DK))
        g_col = jnp.broadcast_to(p["g_col"], (CHUNK, DN_DK))
        g_tot = jnp.broadcast_to(p["g_tot_row"], (CHUNK, CHUNK))
        g_tot = jnp.concatenate([g_tot, g_tot], axis=1)
        decay = jnp.exp(jnp.where(p["incl"], g_col[:, :CHUNK] - p["g_row"], -jnp.inf))
        stage["qk"][slot, i] = (p["qk_kk"][:CHUNK] * decay).astype(BF16)
        p["a"] = jnp.where(p["strict"], beta[:, :CHUNK] * p["qk_kk"][CHUNK:] * decay, 0.0)
        exp_g = jnp.exp(g_col)
        p["x"] = jnp.concatenate([(beta * exp_g) * p["k"], beta * p["v"]], axis=1)
        p["q_dec"] = p["q"] * exp_g
        k_dec = p["k"] * jnp.exp(g_tot - g_col)
        stage["kdt"][slot, i] = k_dec.T.astype(BF16)
        stage["gl"][slot, i] = jnp.exp(g_tot[0:SUBLANES])
    wide = DN_HEADS * CHUNK
    diag = ((lax.broadcasted_iota(jnp.int32, (wide, wide), 0) >> _LOG2_CHUNK)
            == (lax.broadcasted_iota(jnp.int32, (wide, wide), 1) >> _LOG2_CHUNK))

    def block_diag(m):
        m = jnp.concatenate([m] * DN_HEADS, axis=0)
        return jnp.where(diag, m, 0.0).astype(BF16)

    ri = lax.broadcasted_iota(jnp.int32, (CHUNK, wide), 0)
    ci = lax.broadcasted_iota(jnp.int32, (CHUNK, wide), 1) & (CHUNK - 1)
    groups = []
    for g0 in range(0, len(probs), DN_HEADS):
        members = probs[g0:g0 + DN_HEADS]
        groups.append(dict(members=members, first=g0, reverse=members[0]["reverse"],
                           a=jnp.concatenate([p["a"] for p in members], axis=1)))
    for level in range(_LOG2_CHUNK):
        for g in groups:
            lo_i, lo_j = ((ri >> level) & 1) == 1, ((ci >> level) & 1) == 0
            couple = ((ri >> (level + 1)) == (ci >> (level + 1))) & (
                (~lo_i & ~lo_j) if g["reverse"] else (lo_i & lo_j))
            g["l"] = jnp.where(couple, g["a"], 0.0)
        if level == 0:
            for g in groups:
                g["e"] = -g["l"]
            continue
        prods = [jnp.dot(g["l"].astype(BF16), block_diag(g["e"]), preferred_element_type=F32)
                 for g in groups]
        yield
        for g, r in zip(groups, prods):
            g["m"] = g["l"] + r
        prods = [jnp.dot(g["e"].astype(BF16), block_diag(g["m"]), preferred_element_type=F32)
                 for g in groups]
        yield
        for g, r in zip(groups, prods):
            g["e"] = g["e"] - (g["m"] + r)
    low = lax.broadcasted_iota(jnp.int32, (CHUNK, 2 * CHUNK), 1) < CHUNK
    prods = []
    for g in groups:
        for pair in range(DN_HEADS // 2):
            e = g["e"][:, pair * 2 * CHUNK:(pair + 1) * 2 * CHUNK]
            lhs = jnp.concatenate([jnp.where(low, e, 0.0), jnp.where(low, 0.0, e)], axis=0)
            rhs = jnp.concatenate([p["x"] for p in g["members"][2 * pair:2 * pair + 2]], axis=0)
            prods.append(jnp.dot(lhs.astype(BF16), rhs.astype(BF16), preferred_element_type=F32))
    yield
    for i, p in enumerate(probs):
        r = prods[i // 2]
        x = p["x"] + (r[CHUNK:] if i % 2 else r[:CHUNK])
        stage["wq"][slot, i] = jnp.concatenate([x[:, :DN_DK], p["q_dec"]], axis=0).astype(BF16)
        stage["u"][slot, i] = x[:, DN_DK:]


def _dn_scan(out_refs, state_refs, stage, slot):
    states = {(rev, h): state_refs[rev][h] for rev in (False, True) for h in range(DN_HEADS)}
    for step in range(DN_GROUP):
        chains = []
        for reverse in (False, True):
            c = DN_GROUP - 1 - step if reverse else step
            for h in range(DN_HEADS):
                chains.append((reverse, c, h, ((DN_GROUP if reverse else 0) + c) * DN_HEADS + h))
        ws_qs = [jnp.dot(stage["wq"][slot, i], states[rev, h].astype(BF16),
                         preferred_element_type=F32) for rev, c, h, i in chains]
        yield
        v_new = [(stage["u"][slot, i] - r[:CHUNK]).astype(BF16)
                 for (rev, c, h, i), r in zip(chains, ws_qs)]
        intra = [jnp.dot(stage["qk"][slot, i], vn, preferred_element_type=F32)
                 for (rev, c, h, i), vn in zip(chains, v_new)]
        ds = [jnp.dot(stage["kdt"][slot, i], vn, preferred_element_type=F32)
              for (rev, c, h, i), vn in zip(chains, v_new)]
        yield
        for (rev, c, h, i), r, oi, d in zip(chains, ws_qs, intra, ds):
            out_refs[rev][c * CHUNK:(c + 1) * CHUNK, h * DN_DV:(h + 1) * DN_DV] = r[CHUNK:] + oi
            g_last = jnp.concatenate([stage["gl"][slot, i]] * (DN_DK // SUBLANES), axis=0)
            states[rev, h] = states[rev, h] * g_last + d
    for (rev, h), s in states.items():
        state_refs[rev][h] = s


def _dn_kernel(qf_ref, kf_ref, vf_ref, gf_ref, qb_ref, kb_ref, vb_ref, gbb_ref,
               of_ref, ob_ref, sf_ref, sb_ref, wq_ref, u_ref, qk_ref, kdt_ref, gl_ref, *, n):
    t = pl.program_id(0)
    stage = dict(wq=wq_ref, u=u_ref, qk=qk_ref, kdt=kdt_ref, gl=gl_ref)

    @pl.when(t == 0)
    def _():
        for ref in stage.values():
            ref[...] = jnp.zeros_like(ref)

    @pl.when((t == 0) | (lax.rem(t + n - 1, n) == 0))
    def _():
        sf_ref[...] = jnp.zeros_like(sf_ref)
        sb_ref[...] = jnp.zeros_like(sb_ref)

    slot = lax.rem(t, 2)
    prepare = _dn_prepare({False: (qf_ref, kf_ref, vf_ref, gf_ref),
                           True: (qb_ref, kb_ref, vb_ref, gbb_ref)}, stage, slot)
    scan = _dn_scan({False: of_ref, True: ob_ref}, {False: sf_ref, True: sb_ref}, stage, 1 - slot)
    live = [prepare, scan]
    while live:
        for gen in list(live):
            if next(gen, StopIteration) is StopIteration:
                live.remove(gen)


def _delta_rule(qn, kn, v, gb):
    nb, seq, _ = qn.shape
    tg = DN_GROUP * CHUNK
    n = seq // tg
    total = nb * n
    n_prob = 2 * DN_GROUP * DN_HEADS

    def in_map(reverse):
        def index(t):
            tt = jnp.minimum(t, total - 1)
            i = lax.rem(tt, n)
            return (tt // n, n - 1 - i if reverse else i, 0)
        return index

    def out_map(reverse):
        def index(t):
            tt = jnp.maximum(t - 1, 0)
            i = lax.rem(tt, n)
            return (tt // n, n - 1 - i if reverse else i, 0)
        return index

    spec = lambda width, index: pl.BlockSpec((None, tg, width), index)
    widths = (DN_QK_W, DN_QK_W, DN_W, LANES)
    state = pltpu.VMEM((DN_HEADS, DN_DK, DN_DV), F32)
    return pl.pallas_call(
        functools.partial(_dn_kernel, n=n),
        grid=(total + 1,),
        in_specs=[spec(w, in_map(False)) for w in widths] + [spec(w, in_map(True)) for w in widths],
        out_specs=(spec(DN_W, out_map(False)), spec(DN_W, out_map(True))),
        out_shape=(jax.ShapeDtypeStruct((nb, seq, DN_W), F32),
                   jax.ShapeDtypeStruct((nb, seq, DN_W), F32)),
        scratch_shapes=[
            state, state,
            pltpu.VMEM((2, n_prob, DN_DK, DN_DK), BF16),
            pltpu.VMEM((2, n_prob, CHUNK, DN_DV), F32),
            pltpu.VMEM((2, n_prob, CHUNK, CHUNK), BF16),
            pltpu.VMEM((2, n_prob, DN_DK, CHUNK), BF16),
            pltpu.VMEM((2, n_prob, SUBLANES, LANES), F32),
        ],
        compiler_params=_compiler_params(("arbitrary",)),
        name="delta_rule",
    )(qn, kn, v, gb, qn, kn, v, gb)


def _post_kernel(x_ref, oa_ref, of_ref, ob_ref, z_ref, gate_ref, mod_ref, dng_ref, g2_ref,
                 fg_ref, wba_ref, wbd_ref, wo_ref, w1_ref, w2_ref, y_ref, *, final):
    od = of_ref[...] + ob_ref[...]
    z = z_ref[...].astype(F32)
    dng = dng_ref[...]
    parts = []
    for h in range(DN_HEADS):
        lanes = slice(h * DN_DV, (h + 1) * DN_DV)
        parts.append(_rms(od[:, lanes], dng) * _silu(z[:, lanes]))
    od_n = jnp.concatenate(parts, axis=1).astype(BF16)
    br_a = jnp.dot(oa_ref[...], wba_ref[...], preferred_element_type=F32)
    br_d = jnp.dot(od_n, wbd_ref[...], preferred_element_type=F32)
    gate_a = _sigmoid(gate_ref[:, 0:D_MODEL].astype(F32))
    gate_d = _sigmoid(gate_ref[:, D_MODEL:].astype(F32))
    merged = gate_a * br_a + gate_d * br_d
    mix = jnp.dot(merged.astype(BF16), wo_ref[...], preferred_element_type=F32)
    gt1 = mod_ref[:, 2 * D_MODEL:3 * D_MODEL]
    sh2 = mod_ref[:, 3 * D_MODEL:4 * D_MODEL]
    sc2 = mod_ref[:, 4 * D_MODEL:5 * D_MODEL]
    gt2 = mod_ref[:, 5 * D_MODEL:6 * D_MODEL]
    x = x_ref[...] + gt1 * mix
    h2 = (_rms(x, g2_ref[...]) * (1.0 + sc2) + sh2).astype(BF16)
    acc = jnp.zeros_like(x)
    for j in range(D_FF // D_MODEL):
        cols = slice(j * D_MODEL, (j + 1) * D_MODEL)
        hid = jnp.maximum(jnp.dot(h2, w1_ref[:, cols], preferred_element_type=F32), 0.0)
        acc = acc + jnp.dot((hid * hid).astype(BF16), w2_ref[cols, :], preferred_element_type=F32)
    x = x + gt2 * acc
    if final:
        x = _rms(x, fg_ref[...])
    y_ref[...] = x


def _post(x, oa, of, ob, z, gates, mod_l, dn_norm_g, norm_mlp_g, final_g, weights, final):
    nb, seq, _ = x.shape
    tm = TM_POST
    tok = lambda width: pl.BlockSpec((None, tm, width), lambda b, i: (b, i, 0))
    vec = lambda width: _resident((1, width), lambda b, i: (0, 0))
    full = lambda w: _resident(w.shape, lambda b, i: (0, 0))
    return pl.pallas_call(
        functools.partial(_post_kernel, final=final),
        grid=(nb, seq // tm),
        in_specs=[
            tok(D_MODEL), tok(NA_W), tok(DN_W), tok(DN_W), tok(DN_W), tok(N_GATE),
            pl.BlockSpec((None, 1, N_MOD * D_MODEL), lambda b, i: (b, 0, 0)),
            vec(DN_DV), vec(D_MODEL), vec(D_MODEL),
        ] + [full(w) for w in weights],
        out_specs=tok(D_MODEL),
        out_shape=jax.ShapeDtypeStruct((nb, seq, D_MODEL), F32),
        compiler_params=_compiler_params(("parallel", "parallel")),
        name="merge_mlp",
    )(x, oa, of, ob, z, gates, mod_l, dn_norm_g, norm_mlp_g, final_g, *weights)


def _split_w_in(w_in_l):
    o1 = 3 * NA_W
    o2 = o1 + DN_QKV_W
    o3 = o2 + DN_W
    o4 = o3 + AB_W
    wab = jnp.pad(w_in_l[:, o3:o4], ((0, 0), (0, LANES - AB_W)))
    wa = jnp.concatenate([w_in_l[:, :NA_W] * (NA_DH ** -0.5), w_in_l[:, NA_W:o1]], axis=1)
    parts = (wa, w_in_l[:, o1:o2], w_in_l[:, o2:o3], wab, w_in_l[:, o4:])
    return tuple(p.astype(BF16) for p in parts)


def _trunk(x, mod, p):
    for l in range(DEPTH):
        qa, ka, va, qn, kn, v, gb, z, gates = _in_projection(
            x, mod[l], p["norm_mix_g"][l], p["w_in"][l], p["dn_conv"][l], p["dn_a_log"][l],
            p["dn_dt_bias"][l])
        oa = _attention(qa, ka, va, p["bias"][l])
        of, ob = _delta_rule(qn, kn, v, gb)
        x = _post(x, oa, of, ob, z, gates, mod[l], p["dn_norm_g"][l], p["norm_mlp_g"][l],
                  p["final_norm_g"], p["post_w"][l], final=(l == DEPTH - 1))
    return x


def kernel(x_prompt, x_sample, c_prompt, c_sample, norm_mix_g, norm_mlp_g, w_ada, b_ada, w_in,
           na_rpb, dn_conv, dn_a_log, dn_dt_bias, dn_norm_g, w_br_attn, w_br_dn, w_out, w_mlp1,
           w_mlp2, final_norm_g):
    row = lambda t: t.reshape(1, -1).astype(F32)
    p = {
        "norm_mix_g": [row(norm_mix_g[l]) for l in range(DEPTH)],
        "norm_mlp_g": [row(norm_mlp_g[l]) for l in range(DEPTH)],
        "dn_norm_g": [row(dn_norm_g[l]) for l in range(DEPTH)],
        "final_norm_g": row(final_norm_g),
        "w_in": [_split_w_in(w_in[l]) for l in range(DEPTH)],
        "bias": [_attention_bias(na_rpb[l]) for l in range(DEPTH)],
        "dn_conv": [dn_conv[l].astype(F32) for l in range(DEPTH)],
        "dn_a_log": dn_a_log,
        "dn_dt_bias": dn_dt_bias,
        "post_w": [tuple(w[l].astype(BF16) for w in (w_br_attn, w_br_dn, w_out, w_mlp1, w_mlp2))
                   for l in range(DEPTH)],
    }
    n_prompt = c_prompt.shape[0]
    mod = _modulation(jnp.concatenate([c_prompt, c_sample], axis=0), w_ada, b_ada)
    mod = mod.reshape(DEPTH, -1, 1, N_MOD * D_MODEL)
    y_prompt = _trunk(x_prompt, mod[:, :n_prompt], p)
    y_sample = _trunk(x_sample, mod[:, n_prompt:], p)
    return (y_prompt, y_sample)
```

```python
import functools

import jax
import jax.numpy as jnp
import numpy as np
from jax import lax
from jax.experimental import pallas as pl
from jax.experimental.pallas import tpu as pltpu

D_MODEL = 1024
DEPTH = 2
GRID_W = 64
NA_HEADS = 8
NA_DH = 64
NA_W = NA_HEADS * NA_DH
NA_KH = 8
NA_KW = 16
DN_HEADS = 4
DN_DK = 128
DN_DV = 128
DN_QK_W = DN_HEADS * DN_DK
DN_W = DN_HEADS * DN_DV
DN_QKV_W = 2 * DN_QK_W + DN_W
CONV_W = 5
CHUNK = 64
D_FF = 4 * D_MODEL
N_MOD = 6
EPS = 1e-6
N_GATE = 2 * D_MODEL
AB_W = 4 * DN_HEADS
GB_BETA = 0
GB_G = 2 * DN_HEADS
GB_CUM = 4 * DN_HEADS
GB_TOT = 6 * DN_HEADS
GB_END = 8 * DN_HEADS

LANES = 128
SUBLANES = 8
MXU_N = 256
VMEM_LIMIT_BYTES = 56 * 1024 * 1024

TM_IN = 512
TM_POST = 512
ATT_ROWS = 8
ATT_WIN_ROWS = 3 * ATT_ROWS
ATT_UNROLL = 8
DN_GROUP = 8
HALO = SUBLANES

BF16 = jnp.bfloat16
F32 = jnp.float32
HIGHEST = lax.Precision.HIGHEST


def _compiler_params(semantics):
    return pltpu.CompilerParams(dimension_semantics=semantics,
                                vmem_limit_bytes=VMEM_LIMIT_BYTES)


def _resident(block_shape, index_map):
    return pl.BlockSpec(block_shape, index_map, pipeline_mode=pl.Buffered(1))


def _silu(x):
    half = 0.5 * x
    return half * (jnp.tanh(half) + 1.0)


def _sigmoid(x):
    return 0.5 * (jnp.tanh(0.5 * x) + 1.0)


def _rms(x, g):
    return (x * lax.rsqrt(jnp.mean(x * x, axis=-1, keepdims=True) + EPS)) * g


def _mod_kernel(c_ref, w_ref, b_ref, o_ref):
    c = _silu(c_ref[...])
    o_ref[...] = jnp.dot(c, w_ref[...], precision=HIGHEST,
                         preferred_element_type=F32) + b_ref[...]


def _modulation(c, w_ada, b_ada):
    nb = c.shape[0]
    tn = 1024
    n_out = N_MOD * D_MODEL
    return pl.pallas_call(
        _mod_kernel,
        grid=(DEPTH, n_out // tn),
        in_specs=[
            pl.BlockSpec((nb, D_MODEL), lambda l, j: (0, 0)),
            pl.BlockSpec((None, D_MODEL, tn), lambda l, j: (l, 0, j)),
            pl.BlockSpec((None, 1, tn), lambda l, j: (l, 0, j)),
        ],
        out_specs=pl.BlockSpec((None, nb, tn), lambda l, j: (l, 0, j)),
        out_shape=jax.ShapeDtypeStruct((DEPTH, nb, n_out), F32),
        compiler_params=_compiler_params(("arbitrary", "arbitrary")),
        name="adaln_mod",
    )(c, w_ada, b_ada.reshape(DEPTH, 1, n_out))


def _chunk_scan(x, reverse):
    tm = x.shape[0]
    pos = lax.broadcasted_iota(jnp.int32, x.shape, 0) & (CHUNK - 1)
    step = 1
    while step < CHUNK:
        if reverse:
            shifted = pltpu.roll(x, tm - step, axis=0)
            x = x + jnp.where(pos < CHUNK - step, shifted, 0.0)
        else:
            shifted = pltpu.roll(x, step, axis=0)
            x = x + jnp.where(pos >= step, shifted, 0.0)
        step *= 2
    return x


def _in_kernel(x_ref, xprev_ref, xnext_ref, mod_ref, g_ref, wd_ref, wa_ref, wz_ref, wab_ref,
               wg_ref, cw_ref, alog_ref, dtb_ref,
               qa_ref, ka_ref, va_ref, qn_ref, kn_ref, v_ref, gb_ref, z_ref, gate_ref,
               pad_ref, h_ref, *, tm):
    i = pl.program_id(1)
    n = pl.num_programs(1)
    sh1 = mod_ref[:, 0:D_MODEL]
    sc1 = mod_ref[:, D_MODEL:2 * D_MODEL]
    gain = g_ref[...]
    normed = lambda xv: (_rms(xv, gain) * (1.0 + sc1) + sh1).astype(BF16)
    h_ref[...] = normed(x_ref[...])
    h_halo = normed(jnp.concatenate([xprev_ref[...], xnext_ref[...]], axis=0))

    def deltanet_block(c):
        cols = slice(c * MXU_N, (c + 1) * MXU_N)
        w = wd_ref[:, cols]
        d_halo = jnp.dot(h_halo, w, preferred_element_type=F32)
        pad_ref[0:HALO, cols] = jnp.where(i > 0, d_halo[0:HALO], 0.0)
        pad_ref[HALO + tm:, cols] = jnp.where(i < n - 1, d_halo[HALO:], 0.0)
        pad_ref[HALO:HALO + tm, cols] = jnp.dot(h_ref[...], w, preferred_element_type=F32)
        first = HALO - CONV_W // 2
        acc = pad_ref[first:first + tm, cols] * cw_ref[0:1, cols]
        for t in range(1, CONV_W):
            acc = acc + pad_ref[first + t:first + t + tm, cols] * cw_ref[t:t + 1, cols]
        y = _silu(acc)
        out_ref, scale = ((qn_ref, DN_DK ** -0.5), (kn_ref, None), (v_ref, None))[c * MXU_N // DN_QK_W]
        base = c * MXU_N % DN_QK_W
        for hd in range(MXU_N // DN_DK):
            yh = y[:, hd * DN_DK:(hd + 1) * DN_DK]
            if out_ref is not v_ref:
                yh = yh * lax.rsqrt(jnp.sum(yh * yh, axis=-1, keepdims=True) + EPS)
            if scale is not None:
                yh = yh * scale
            out_ref[:, base + hd * DN_DK:base + (hd + 1) * DN_DK] = yh

    def plain_block(w_ref, out_ref, c):
        cols = slice(c * MXU_N, (c + 1) * MXU_N)
        r = jnp.dot(h_ref[...], w_ref[:, cols], preferred_element_type=F32)
        out_ref[:, cols] = r.astype(BF16)

    others = [functools.partial(plain_block, w_ref, out_ref, c)
              for w_ref, out_ref, width in ((wa_ref, qa_ref, NA_W),
                                            (wa_ref.at[:, NA_W:2 * NA_W], ka_ref, NA_W),
                                            (wa_ref.at[:, 2 * NA_W:], va_ref, NA_W),
                                            (wz_ref, z_ref, DN_W), (wg_ref, gate_ref, N_GATE))
              for c in range(width // MXU_N)]
    ab = jnp.dot(h_ref[...], wab_ref[...], preferred_element_type=F32)
    n_dn = DN_QKV_W // MXU_N
    per_dn = 1
    for c in range(n_dn):
        deltanet_block(c)
        for emit in others[c * per_dn:(c + 1) * per_dn]:
            emit()
    for emit in others[n_dn * per_dn:]:
        emit()

    beta = _sigmoid(ab)
    a = ab + dtb_ref[...]
    softplus = jnp.maximum(a, 0.0) + jnp.log1p(jnp.exp(-jnp.abs(a)))
    g = -jnp.exp(alog_ref[...]) * softplus
    col = lax.broadcasted_iota(jnp.int32, ab.shape, 1)
    raw = jnp.where(col < GB_G, beta, jnp.where(col < GB_CUM, g, 0.0))
    prefix = _chunk_scan(raw, reverse=False)
    suffix = _chunk_scan(raw, reverse=True)
    total = pltpu.roll(prefix + suffix - raw, GB_TOT - GB_G, axis=1)
    cum = jnp.where(col < GB_CUM + DN_HEADS, pltpu.roll(prefix, GB_CUM - GB_G, axis=1),
                    pltpu.roll(suffix, GB_CUM - GB_G, axis=1))
    gb_ref[...] = jnp.where(col < GB_CUM, raw,
                            jnp.where(col < GB_TOT, cum, jnp.where(col < GB_END, total, 0.0)))


def _in_projection(x, mod_l, norm_g, w_parts, conv_w, a_log, dt_bias):
    nb, seq, _ = x.shape
    tm = TM_IN
    per = tm // HALO
    n_halo = seq // HALO
    wa, wd, wz, wab, wg = w_parts
    pad_cols = lambda t: jnp.pad(t.reshape(1, 2 * DN_HEADS).astype(F32),
                                 ((0, 0), (GB_G, LANES - GB_CUM)))
    tok = lambda width: pl.BlockSpec((None, tm, width), lambda b, i: (b, i, 0))
    full = lambda w: _resident(w.shape, lambda b, i: (0, 0))
    vec = lambda width: _resident((1, width), lambda b, i: (0, 0))
    out_shape = (
        jax.ShapeDtypeStruct((nb, seq, NA_W), BF16),
        jax.ShapeDtypeStruct((nb, seq, NA_W), BF16),
        jax.ShapeDtypeStruct((nb, seq, NA_W), BF16),
        jax.ShapeDtypeStruct((nb, seq, DN_QK_W), F32),
        jax.ShapeDtypeStruct((nb, seq, DN_QK_W), F32),
        jax.ShapeDtypeStruct((nb, seq, DN_W), F32),
        jax.ShapeDtypeStruct((nb, seq, LANES), F32),
        jax.ShapeDtypeStruct((nb, seq, DN_W), BF16),
        jax.ShapeDtypeStruct((nb, seq, N_GATE), BF16),
    )
    return pl.pallas_call(
        functools.partial(_in_kernel, tm=tm),
        grid=(nb, seq // tm),
        in_specs=[
            tok(D_MODEL),
            pl.BlockSpec((None, HALO, D_MODEL), lambda b, i: (b, jnp.maximum(i * per - 1, 0), 0)),
            pl.BlockSpec((None, HALO, D_MODEL),
                         lambda b, i: (b, jnp.minimum((i + 1) * per, n_halo - 1), 0)),
            pl.BlockSpec((None, 1, N_MOD * D_MODEL), lambda b, i: (b, 0, 0)),
            vec(D_MODEL),
            full(wd), full(wa), full(wz), full(wab), full(wg),
            _resident((CONV_W, DN_QKV_W), lambda b, i: (0, 0)),
            vec(LANES), vec(LANES),
        ],
        out_specs=(tok(NA_W), tok(NA_W), tok(NA_W), tok(DN_QK_W), tok(DN_QK_W), tok(DN_W),
                   tok(LANES), tok(DN_W), tok(N_GATE)),
        out_shape=out_shape,
        scratch_shapes=[pltpu.VMEM((tm + 2 * HALO, DN_QKV_W), F32),
                        pltpu.VMEM((tm, D_MODEL), BF16)],
        compiler_params=_compiler_params(("parallel", "parallel")),
        name="in_projection",
    )(x, x, x, mod_l, norm_g, wd, wa, wz, wab, wg, conv_w, pad_cols(a_log), pad_cols(dt_bias))


def _att_kernel(q_ref, k_ref, v_ref, bias_ref, o_ref, s_ref, *, n_blocks):
    j = pl.program_id(1)
    rows = n_blocks * ATT_ROWS
    base = jnp.clip(j - 1, 0, n_blocks - 3)
    lane = lax.broadcasted_iota(jnp.int32, (GRID_W, LANES), 1)
    low = lane < NA_DH

    def rows_body(it, carry):
        probs = []
        for u in range(ATT_UNROLL):
            ri = it * ATT_UNROLL + u
            r = j * ATT_ROWS + ri
            sr = jnp.clip(r - NA_KH // 2, 0, rows - NA_KH)
            delta = r - sr
            off = pl.multiple_of((sr - base * ATT_ROWS) * GRID_W, GRID_W)
            qrow = pl.multiple_of(ri * GRID_W, GRID_W)
            for p in range(NA_HEADS // 2):
                lanes = slice(p * LANES, (p + 1) * LANES)
                qp = q_ref[pl.ds(qrow, GRID_W), lanes]
                zero = jnp.zeros_like(qp)
                lhs = jnp.concatenate([jnp.where(low, qp, zero), jnp.where(low, zero, qp)], axis=0)
                kp = k_ref[0, pl.ds(off, NA_KH * GRID_W), lanes]
                s_ref[len(probs)] = lax.dot_general(lhs, kp, (((1,), (1,)), ((), ())),
                                                    preferred_element_type=F32)
                probs.append((qrow, off, delta, p, lanes, len(probs)))
        pvs = []
        for qrow, off, delta, p, lanes, slot in probs:
            s = s_ref[slot] + bias_ref[delta, p]
            m = jnp.max(s, axis=-1, keepdims=True)
            e = jnp.exp(s - m)
            den = jnp.sum(e, axis=-1, keepdims=True)
            vp = v_ref[0, pl.ds(off, NA_KH * GRID_W), lanes]
            pvs.append((jnp.dot(e.astype(BF16), vp, preferred_element_type=F32), den))
        for (qrow, off, delta, p, lanes, slot), (pv, den) in zip(probs, pvs):
            pv = pv / den
            o_ref[pl.ds(qrow, GRID_W), lanes] = jnp.where(low, pv[0:GRID_W], pv[GRID_W:]).astype(BF16)
        return carry

    lax.fori_loop(0, ATT_ROWS // ATT_UNROLL, rows_body, 0)


def _attention(qa, ka, va, bias):
    nb, seq, _ = qa.shape
    blk = ATT_ROWS * GRID_W
    n_blocks = seq // blk
    assert n_blocks >= 3
    win = ATT_WIN_ROWS * GRID_W

    def win_map(b, j):
        return (b, jnp.clip(j - 1, 0, n_blocks - 3) * blk, 0)

    win_spec = pl.BlockSpec((pl.Element(1), pl.Element(win), pl.Element(NA_W)), win_map)
    return pl.pallas_call(
        functools.partial(_att_kernel, n_blocks=n_blocks),
        grid=(nb, n_blocks),
        in_specs=[
            pl.BlockSpec((None, blk, NA_W), lambda b, j: (b, j, 0)),
            win_spec, win_spec,
            _resident(bias.shape, lambda b, j: (0, 0, 0, 0)),
        ],
        out_specs=pl.BlockSpec((None, blk, NA_W), lambda b, j: (b, j, 0)),
        out_shape=jax.ShapeDtypeStruct((nb, seq, NA_W), BF16),
        scratch_shapes=[pltpu.VMEM((ATT_UNROLL * NA_HEADS // 2, 2 * GRID_W, NA_KH * GRID_W), F32)],
        compiler_params=_compiler_params(("parallel", "parallel")),
        name="neighborhood_attention",
    )(qa, ka, va, bias)


def _attention_bias(rpb):
    qc = np.arange(GRID_W)
    kc = np.arange(GRID_W)
    start = np.clip(qc - NA_KW // 2, 0, GRID_W - NA_KW)
    valid = (kc[None, :] >= start[:, None]) & (kc[None, :] < start[:, None] + NA_KW)
    neg = jnp.asarray(np.where(valid, 0.0, -1e30).astype(np.float32))
    dc = np.clip(kc[None, :] - qc[:, None] + NA_KW - 1, 0, 2 * NA_KW - 2)
    delta = np.arange(NA_KH)
    dr = np.arange(NA_KH)[None, :] - delta[:, None] + (NA_KH - 1)
    onehot = jnp.asarray((dc[None] == np.arange(2 * NA_KW - 1)[:, None, None]).astype(np.float32))
    t = jnp.einsum('hdic,cqk->dhqik', rpb.astype(F32)[:, dr], onehot, precision=HIGHEST)
    t = t + neg[None, None, :, None, :]
    return t.reshape(NA_KH, NA_HEADS // 2, 2 * GRID_W, NA_KH * GRID_W)


_LOG2_CHUNK = CHUNK.bit_length() - 1


def _dn_prepare(refs, stage, slot):
    row = lax.broadcasted_iota(jnp.int32, (CHUNK, CHUNK), 0)
    colm = lax.broadcasted_iota(jnp.int32, (CHUNK, CHUNK), 1)
    nt = (((1,), (1,)), ((), ()))
    probs = []
    for reverse in (False, True):
        q_ref, k_ref, v_ref, g_ref = refs[reverse]
        incl, strict = (colm >= row, colm > row) if reverse else (colm <= row, colm < row)
        for c in range(DN_GROUP):
            rows = slice(c * CHUNK, (c + 1) * CHUNK)
            gb = g_ref[rows, :]
            gb_t = gb.T
            for h in range(DN_HEADS):
                lanes = slice(h * DN_DK, (h + 1) * DN_DK)
                hd = h + (DN_HEADS if reverse else 0)
                q, k, v = q_ref[rows, lanes], k_ref[rows, lanes], v_ref[rows, lanes]
                q_b, k_b = q.astype(BF16), k.astype(BF16)
                probs.append(dict(
                    reverse=reverse, incl=incl, strict=strict, q=q, k=k, v=v,
                    beta=gb[:, GB_BETA + hd:GB_BETA + hd + 1],
                    g_col=gb[:, GB_CUM + hd:GB_CUM + hd + 1],
                    g_row=gb_t[GB_CUM + hd:GB_CUM + hd + 1, :],
                    g_tot_row=gb_t[GB_TOT + hd:GB_TOT + hd + 1, :],
                    qk_kk=lax.dot_general(jnp.concatenate([q_b, k_b], axis=0), k_b, nt,
                                          preferred_element_type=F32)))
    yield
    for i, p in enumerate(probs):
        beta = jnp.broadcast_to(p["beta"], (CHUNK, DN_DK))
        g_col = jnp.broadcast_to(p["g_col"], (CHUNK, DN_DK))
        g_tot = jnp.broadcast_to(p["g_tot_row"], (CHUNK, CHUNK))
        g_tot = jnp.concatenate([g_tot, g_tot], axis=1)
        decay = jnp.exp(jnp.where(p["incl"], g_col[:, :CHUNK] - p["g_row"], -jnp.inf))
        stage["qk"][slot, i] = (p["qk_kk"][:CHUNK] * decay).astype(BF16)
        p["a"] = jnp.where(p["strict"], beta[:, :CHUNK] * p["qk_kk"][CHUNK:] * decay, 0.0)
        exp_g = jnp.exp(g_col)
        p["x"] = jnp.concatenate([(beta * exp_g) * p["k"], beta * p["v"]], axis=1)
        p["q_dec"] = p["q"] * exp_g
        k_dec = p["k"] * jnp.exp(g_tot - g_col)
        stage["kdt"][slot, i] = k_dec.T.astype(BF16)
        stage["gl"][slot, i] = jnp.exp(g_tot[0:SUBLANES])
    wide = DN_HEADS * CHUNK
    diag = ((lax.broadcasted_iota(jnp.int32, (wide, wide), 0) >> _LOG2_CHUNK)
            == (lax.broadcasted_iota(jnp.int32, (wide, wide), 1) >> _LOG2_CHUNK))

    def block_diag(m):
        m = jnp.concatenate([m] * DN_HEADS, axis=0)
        return jnp.where(diag, m, 0.0).astype(BF16)

    ri = lax.broadcasted_iota(jnp.int32, (CHUNK, wide), 0)
    ci = lax.broadcasted_iota(jnp.int32, (CHUNK, wide), 1) & (CHUNK - 1)
    groups = []
    for g0 in range(0, len(probs), DN_HEADS):
        members = probs[g0:g0 + DN_HEADS]
        groups.append(dict(members=members, first=g0, reverse=members[0]["reverse"],
                           a=jnp.concatenate([p["a"] for p in members], axis=1)))
    low = lax.broadcasted_iota(jnp.int32, (CHUNK, 2 * CHUNK), 1) < CHUNK

    def solve(wave):
        for level in range(_LOG2_CHUNK):
            for g in wave:
                lo_i, lo_j = ((ri >> level) & 1) == 1, ((ci >> level) & 1) == 0
                couple = ((ri >> (level + 1)) == (ci >> (level + 1))) & (
                    (~lo_i & ~lo_j) if g["reverse"] else (lo_i & lo_j))
                g["l"] = jnp.where(couple, g["a"], 0.0)
            if level == 0:
                for g in wave:
                    g["e"] = -g["l"]
                continue
            prods = [jnp.dot(g["l"].astype(BF16), block_diag(g["e"]), preferred_element_type=F32)
                     for g in wave]
            yield
            for g, r in zip(wave, prods):
                g["m"] = g["l"] + r
            prods = [jnp.dot(g["e"].astype(BF16), block_diag(g["m"]), preferred_element_type=F32)
                     for g in wave]
            yield
            for g, r in zip(wave, prods):
                g["e"] = g["e"] - (g["m"] + r)
        prods = []
        for g in wave:
            for pair in range(DN_HEADS // 2):
                e = g["e"][:, pair * 2 * CHUNK:(pair + 1) * 2 * CHUNK]
                lhs = jnp.concatenate([jnp.where(low, e, 0.0), jnp.where(low, 0.0, e)], axis=0)
                rhs = jnp.concatenate([p["x"] for p in g["members"][2 * pair:2 * pair + 2]], axis=0)
                prods.append(jnp.dot(lhs.astype(BF16), rhs.astype(BF16),
                                     preferred_element_type=F32))
        yield
        for gi, g in enumerate(wave):
            for h, p in enumerate(g["members"]):
                i = g["first"] + h
                r = prods[gi * (DN_HEADS // 2) + h // 2]
                x = p["x"] + (r[CHUNK:] if h % 2 else r[:CHUNK])
                stage["wq"][slot, i] = jnp.concatenate([x[:, :DN_DK], p["q_dec"]],
                                                       axis=0).astype(BF16)
                stage["u"][slot, i] = x[:, DN_DK:]

    half = len(groups) // 2
    yield from solve(groups[:half])
    yield from solve(groups[half:])


def _dn_scan(out_refs, state_refs, stage, slot):
    states = {(rev, h): state_refs[rev][h] for rev in (False, True) for h in range(DN_HEADS)}
    for step in range(DN_GROUP):
        chains = []
        for reverse in (False, True):
            c = DN_GROUP - 1 - step if reverse else step
            for h in range(DN_HEADS):
                chains.append((reverse, c, h, ((DN_GROUP if reverse else 0) + c) * DN_HEADS + h))
        ws_qs = [jnp.dot(stage["wq"][slot, i], states[rev, h].astype(BF16),
                         preferred_element_type=F32) for rev, c, h, i in chains]
        yield
        v_new = [(stage["u"][slot, i] - r[:CHUNK]).astype(BF16)
                 for (rev, c, h, i), r in zip(chains, ws_qs)]
        intra = [jnp.dot(stage["qk"][slot, i], vn, preferred_element_type=F32)
                 for (rev, c, h, i), vn in zip(chains, v_new)]
        ds = [jnp.dot(stage["kdt"][slot, i], vn, preferred_element_type=F32)
              for (rev, c, h, i), vn in zip(chains, v_new)]
        yield
        for (rev, c, h, i), r, oi, d in zip(chains, ws_qs, intra, ds):
            out_refs[rev][c * CHUNK:(c + 1) * CHUNK, h * DN_DV:(h + 1) * DN_DV] = r[CHUNK:] + oi
            g_last = jnp.concatenate([stage["gl"][slot, i]] * (DN_DK // SUBLANES), axis=0)
            states[rev, h] = states[rev, h] * g_last + d
    for (rev, h), s in states.items():
        state_refs[rev][h] = s


def _dn_kernel(qf_ref, kf_ref, vf_ref, gf_ref, qb_ref, kb_ref, vb_ref, gbb_ref,
               of_ref, ob_ref, sf_ref, sb_ref, wq_ref, u_ref, qk_ref, kdt_ref, gl_ref, *, n):
    t = pl.program_id(0)
    stage = dict(wq=wq_ref, u=u_ref, qk=qk_ref, kdt=kdt_ref, gl=gl_ref)

    @pl.when(t == 0)
    def _():
        for ref in stage.values():
            ref[...] = jnp.zeros_like(ref)

    @pl.when((t == 0) | (lax.rem(t + n - 1, n) == 0))
    def _():
        sf_ref[...] = jnp.zeros_like(sf_ref)
        sb_ref[...] = jnp.zeros_like(sb_ref)

    slot = lax.rem(t, 2)
    prepare = _dn_prepare({False: (qf_ref, kf_ref, vf_ref, gf_ref),
                           True: (qb_ref, kb_ref, vb_ref, gbb_ref)}, stage, slot)
    scan = _dn_scan({False: of_ref, True: ob_ref}, {False: sf_ref, True: sb_ref}, stage, 1 - slot)
    live = [prepare, scan]
    while live:
        for gen in list(live):
            if next(gen, StopIteration) is StopIteration:
                live.remove(gen)


def _delta_rule(qn, kn, v, gb):
    nb, seq, _ = qn.shape
    tg = DN_GROUP * CHUNK
    n = seq // tg
    total = nb * n
    n_prob = 2 * DN_GROUP * DN_HEADS

    def in_map(reverse):
        def index(t):
            tt = jnp.minimum(t, total - 1)
            i = lax.rem(tt, n)
            return (tt // n, n - 1 - i if reverse else i, 0)
        return index

    def out_map(reverse):
        def index(t):
            tt = jnp.maximum(t - 1, 0)
            i = lax.rem(tt, n)
            return (tt // n, n - 1 - i if reverse else i, 0)
        return index

    spec = lambda width, index: pl.BlockSpec((None, tg, width), index)
    widths = (DN_QK_W, DN_QK_W, DN_W, LANES)
    state = pltpu.VMEM((DN_HEADS, DN_DK, DN_DV), F32)
    return pl.pallas_call(
        functools.partial(_dn_kernel, n=n),
        grid=(total + 1,),
        in_specs=[spec(w, in_map(False)) for w in widths] + [spec(w, in_map(True)) for w in widths],
        out_specs=(spec(DN_W, out_map(False)), spec(DN_W, out_map(True))),
        out_shape=(jax.ShapeDtypeStruct((nb, seq, DN_W), F32),
                   jax.ShapeDtypeStruct((nb, seq, DN_W), F32)),
        scratch_shapes=[
            state, state,
            pltpu.VMEM((2, n_prob, DN_DK, DN_DK), BF16),
            pltpu.VMEM((2, n_prob, CHUNK, DN_DV), F32),
            pltpu.VMEM((2, n_prob, CHUNK, CHUNK), BF16),
            pltpu.VMEM((2, n_prob, DN_DK, CHUNK), BF16),
            pltpu.VMEM((2, n_prob, SUBLANES, LANES), F32),
        ],
        compiler_params=_compiler_params(("arbitrary",)),
        name="delta_rule",
    )(qn, kn, v, gb, qn, kn, v, gb)


def _post_kernel(x_ref, oa_ref, of_ref, ob_ref, z_ref, gate_ref, mod_ref, dng_ref, g2_ref,
                 fg_ref, wba_ref, wbd_ref, wo_ref, w1_ref, w2_ref, y_ref, *, final):
    od = of_ref[...] + ob_ref[...]
    z = z_ref[...].astype(F32)
    dng = dng_ref[...]
    parts = []
    for h in range(DN_HEADS):
        lanes = slice(h * DN_DV, (h + 1) * DN_DV)
        parts.append(_rms(od[:, lanes], dng) * _silu(z[:, lanes]))
    od_n = jnp.concatenate(parts, axis=1).astype(BF16)
    br_a = jnp.dot(oa_ref[...], wba_ref[...], preferred_element_type=F32)
    br_d = jnp.dot(od_n, wbd_ref[...], preferred_element_type=F32)
    gate_a = _sigmoid(gate_ref[:, 0:D_MODEL].astype(F32))
    gate_d = _sigmoid(gate_ref[:, D_MODEL:].astype(F32))
    merged = gate_a * br_a + gate_d * br_d
    mix = jnp.dot(merged.astype(BF16), wo_ref[...], preferred_element_type=F32)
    gt1 = mod_ref[:, 2 * D_MODEL:3 * D_MODEL]
    sh2 = mod_ref[:, 3 * D_MODEL:4 * D_MODEL]
    sc2 = mod_ref[:, 4 * D_MODEL:5 * D_MODEL]
    gt2 = mod_ref[:, 5 * D_MODEL:6 * D_MODEL]
    x = x_ref[...] + gt1 * mix
    h2 = (_rms(x, g2_ref[...]) * (1.0 + sc2) + sh2).astype(BF16)
    acc = jnp.zeros_like(x)
    for j in range(D_FF // D_MODEL):
        cols = slice(j * D_MODEL, (j + 1) * D_MODEL)
        hid = jnp.maximum(jnp.dot(h2, w1_ref[:, cols], preferred_element_type=F32), 0.0)
        acc = acc + jnp.dot((hid * hid).astype(BF16), w2_ref[cols, :], preferred_element_type=F32)
    x = x + gt2 * acc
    if final:
        x = _rms(x, fg_ref[...])
    y_ref[...] = x


def _post(x, oa, of, ob, z, gates, mod_l, dn_norm_g, norm_mlp_g, final_g, weights, final):
    nb, seq, _ = x.shape
    tm = TM_POST
    tok = lambda width: pl.BlockSpec((None, tm, width), lambda b, i: (b, i, 0))
    vec = lambda width: _resident((1, width), lambda b, i: (0, 0))
    full = lambda w: _resident(w.shape, lambda b, i: (0, 0))
    return pl.pallas_call(
        functools.partial(_post_kernel, final=final),
        grid=(nb, seq // tm),
        in_specs=[
            tok(D_MODEL), tok(NA_W), tok(DN_W), tok(DN_W), tok(DN_W), tok(N_GATE),
            pl.BlockSpec((None, 1, N_MOD * D_MODEL), lambda b, i: (b, 0, 0)),
            vec(DN_DV), vec(D_MODEL), vec(D_MODEL),
        ] + [full(w) for w in weights],
        out_specs=tok(D_MODEL),
        out_shape=jax.ShapeDtypeStruct((nb, seq, D_MODEL), F32),
        compiler_params=_compiler_params(("parallel", "parallel")),
        name="merge_mlp",
    )(x, oa, of, ob, z, gates, mod_l, dn_norm_g, norm_mlp_g, final_g, *weights)


def _split_w_in(w_in_l):
    o1 = 3 * NA_W
    o2 = o1 + DN_QKV_W
    o3 = o2 + DN_W
    o4 = o3 + AB_W
    wab = jnp.pad(w_in_l[:, o3:o4], ((0, 0), (0, LANES - AB_W)))
    wa = jnp.concatenate([w_in_l[:, :NA_W] * (NA_DH ** -0.5), w_in_l[:, NA_W:o1]], axis=1)
    parts = (wa, w_in_l[:, o1:o2], w_in_l[:, o2:o3], wab, w_in_l[:, o4:])
    return tuple(p.astype(BF16) for p in parts)


def _trunk(x, mod, p):
    for l in range(DEPTH):
        qa, ka, va, qn, kn, v, gb, z, gates = _in_projection(
            x, mod[l], p["norm_mix_g"][l], p["w_in"][l], p["dn_conv"][l], p["dn_a_log"][l],
            p["dn_dt_bias"][l])
        oa = _attention(qa, ka, va, p["bias"][l])
        of, ob = _delta_rule(qn, kn, v, gb)
        x = _post(x, oa, of, ob, z, gates, mod[l], p["dn_norm_g"][l], p["norm_mlp_g"][l],
                  p["final_norm_g"], p["post_w"][l], final=(l == DEPTH - 1))
    return x


def kernel(x_prompt, x_sample, c_prompt, c_sample, norm_mix_g, norm_mlp_g, w_ada, b_ada, w_in,
           na_rpb, dn_conv, dn_a_log, dn_dt_bias, dn_norm_g, w_br_attn, w_br_dn, w_out, w_mlp1,
           w_mlp2, final_norm_g):
    row = lambda t: t.reshape(1, -1).astype(F32)
    p = {
        "norm_mix_g": [row(norm_mix_g[l]) for l in range(DEPTH)],
        "norm_mlp_g": [row(norm_mlp_g[l]) for l in range(DEPTH)],
        "dn_norm_g": [row(dn_norm_g[l]) for l in range(DEPTH)],
        "final_norm_g": row(final_norm_g),
        "w_in": [_split_w_in(w_in[l]) for l in range(DEPTH)],
        "bias": [_attention_bias(na_rpb[l]) for l in range(DEPTH)],
        "dn_conv": [dn_conv[l].astype(F32) for l in range(DEPTH)],
        "dn_a_log": dn_a_log,
        "dn_dt_bias": dn_dt_bias,
        "post_w": [tuple(w[l].astype(BF16) for w in (w_br_attn, w_br_dn, w_out, w_mlp1, w_mlp2))
                   for l in range(DEPTH)],
    }
    n_prompt = c_prompt.shape[0]
    mod = _modulation(jnp.concatenate([c_prompt, c_sample], axis=0), w_ada, b_ada)
    mod = mod.reshape(DEPTH, -1, 1, N_MOD * D_MODEL)
    y_prompt = _trunk(x_prompt, mod[:, :n_prompt], p)
    y_sample = _trunk(x_sample, mod[:, n_prompt:], p)
    return (y_prompt, y_sample)
```
